```python
import math
import jax, jax.numpy as jnp
from jax import lax
import numpy as np

D_MODEL = 1024
BATCH = 8
SEQ = 2048
DEPTH = 4
DEC_BATCH = 128
DEC_SEQ = 1
PAST_LEN = 16384
PAGE_SIZE = 128

CONF_W = D_MODEL
CONF_KERNEL = 31
SSD_HEAD_DIM = 64
SSD_HEADS = D_MODEL // SSD_HEAD_DIM
SSD_W = SSD_HEADS * SSD_HEAD_DIM
SSD_GROUPS = 2
SSD_STATE = 128
SSD_CONV = 4
SSD_CHUNK = 128
XBC_W = SSD_W + 2 * SSD_GROUPS * SSD_STATE
D_MIX = CONF_W + SSD_W
D_IN = 2 * CONF_W + SSD_W + XBC_W + SSD_HEADS
SPLIT_IN = (CONF_W, 2 * CONF_W, 2 * CONF_W + SSD_W, 2 * CONF_W + SSD_W + XBC_W)
SPLIT_XBC = (SSD_W, SSD_W + SSD_GROUPS * SSD_STATE)
MEM_LEN = 256
XA_HEADS = 4
XA_HEAD_DIM = D_MODEL // XA_HEADS
XA_W = XA_HEADS * XA_HEAD_DIM
D_FF = 11 * D_MODEL // 4
FFN_CONV = 3
ALPHA = (2.0 * DEPTH) ** 0.25
BETA = (8.0 * DEPTH) ** -0.25
LN_EPS = 1e-5

kernel_name = "hymba_conformer_ssd_memxattn_convffn_step"


def layer_norm(x, g, b):
    xf = x.astype(jnp.float32)
    mu = jnp.mean(xf, -1, keepdims=True)
    var = jnp.mean(jnp.square(xf - mu), -1, keepdims=True)
    y = (xf - mu) * lax.rsqrt(var + LN_EPS) * g.astype(jnp.float32) + b.astype(jnp.float32)
    return y.astype(x.dtype)


def rms_norm_f32(xf, g):
    ms = jnp.mean(jnp.square(xf), -1, keepdims=True)
    return xf * lax.rsqrt(ms + LN_EPS) * g.astype(jnp.float32)


def causal_dwconv(x, prev, w, b):
    k = w.shape[0]
    xp = jnp.concatenate([prev.astype(x.dtype), x], axis=1)
    y = lax.conv_general_dilated(xp, w[:, None, :].astype(x.dtype), window_strides=(1,),
                                 padding='VALID', dimension_numbers=('NWC', 'WIO', 'NWC'),
                                 feature_group_count=x.shape[-1])
    return y + b.astype(x.dtype), xp[:, xp.shape[1] - (k - 1):]


def ssd_chunked(x, dt, a, bm, cm, h0):
    bsz, t, nh, p = x.shape
    g = bm.shape[2]
    f32 = jnp.float32
    x = x.astype(f32)
    bh = jnp.repeat(bm.astype(f32), nh // g, axis=2)
    ch = jnp.repeat(cm.astype(f32), nh // g, axis=2)
    q = min(SSD_CHUNK, t)
    nc = -(-t // q)
    pad = nc * q - t

    def chunk(z):
        z = jnp.pad(z, [(0, 0), (0, pad)] + [(0, 0)] * (z.ndim - 2))
        return z.reshape((bsz, nc, q) + z.shape[2:])

    dtc = chunk(dt)
    xc, bc, cc = chunk(x), chunk(bh), chunk(ch)
    a_cum = jnp.cumsum(dtc * a, axis=2)
    dtx = xc * dtc[..., None]
    seg = a_cum[:, :, :, None, :] - a_cum[:, :, None, :, :]
    causal = jnp.tril(jnp.ones((q, q), bool))[None, None, :, :, None]
    decay = jnp.exp(jnp.where(causal, seg, -jnp.inf))
    scores = jnp.einsum('bclhn,bcshn->bclsh', cc, bc) * decay
    y_diag = jnp.einsum('bclsh,bcshp->bclhp', scores, dtx)
    decay_to_end = jnp.exp(a_cum[:, :, -1:, :] - a_cum)
    chunk_states = jnp.einsum('bclhn,bclh,bclhp->bchpn', bc, decay_to_end, dtx)
    chunk_decay = jnp.exp(a_cum[:, :, -1, :])

    def step(h, inp):
        s, d = inp
        return h * d[:, :, None, None] + s, h

    h_last, h_prev = lax.scan(step, h0.astype(f32),
                              (jnp.moveaxis(chunk_states, 1, 0), jnp.moveaxis(chunk_decay, 1, 0)))
    h_prev = jnp.moveaxis(h_prev, 0, 1)
    y_off = jnp.einsum('bclhn,bchpn,bclh->bclhp', cc, h_prev, jnp.exp(a_cum))
    y = (y_diag + y_off).reshape(bsz, nc * q, nh, p)[:, :t]
    return y, h_last


def decoder_layer(x, mk, mv, conf_st, ssdc_st, ssd_st, ffn_st, p):
    bsz, t, _ = x.shape
    h = x @ p["w_in"]
    glu_v, glu_g, z, xbc, dt_raw = jnp.split(h, SPLIT_IN, axis=-1)
    a = glu_v * jax.nn.sigmoid(glu_g)
    a, new_conf = causal_dwconv(a, conf_st, p["conf_conv_w"], p["conf_conv_b"])
    a = jax.nn.silu(layer_norm(a, p["conf_ln_g"], p["conf_ln_b"]))
    xbc, new_ssdc = causal_dwconv(xbc, ssdc_st, p["ssd_conv_w"], p["ssd_conv_b"])
    xbc = jax.nn.silu(xbc)
    xs, bm, cm = jnp.split(xbc, SPLIT_XBC, axis=-1)
    xs = xs.reshape(bsz, t, SSD_HEADS, SSD_HEAD_DIM)
    bm = bm.reshape(bsz, t, SSD_GROUPS, SSD_STATE)
    cm = cm.reshape(bsz, t, SSD_GROUPS, SSD_STATE)
    dt = jax.nn.softplus(dt_raw.astype(jnp.float32) + p["ssd_dt_bias"].astype(jnp.float32))
    a_neg = -jnp.exp(p["ssd_a_log"].astype(jnp.float32))
    y, new_ssd = ssd_chunked(xs, dt, a_neg, bm, cm, ssd_st)
    y = y + p["ssd_d"].astype(jnp.float32)[:, None] * xs.astype(jnp.float32)
    y = y.reshape(bsz, t, SSD_W) * jax.nn.silu(z.astype(jnp.float32))
    y = rms_norm_f32(y, p["ssd_norm_g"]).astype(x.dtype)
    mix = jnp.concatenate([a, y], axis=-1) @ p["w_out"]
    x = layer_norm(ALPHA * x + mix, p["ln_mix_g"], p["ln_mix_b"])
    q = (x @ p["xa_wq"]).reshape(bsz, t, XA_HEADS, XA_HEAD_DIM)
    s = jnp.einsum('bthd,bmhd->bhtm', q, mk.astype(x.dtype)).astype(jnp.float32) * (XA_HEAD_DIM ** -0.5)
    pr = jax.nn.softmax(s, axis=-1).astype(x.dtype)
    o = jnp.einsum('bhtm,bmhd->bthd', pr, mv.astype(x.dtype)).reshape(bsz, t, XA_W)
    x = layer_norm(ALPHA * x + o @ p["xa_wo"], p["ln_xa_g"], p["ln_xa_b"])
    u = x @ p["ffn_w_up"]
    u, new_ffn = causal_dwconv(u, ffn_st, p["ffn_conv_w"], p["ffn_conv_b"])
    uv, ug = jnp.split(u, 2, axis=-1)
    f = (jax.nn.silu(ug) * uv) @ p["ffn_w_down"]
    x = layer_norm(ALPHA * x + f, p["ln_ffn_g"], p["ln_ffn_b"])
    return x, (new_conf, new_ssdc, new_ssd.astype(ssd_st.dtype), new_ffn)


def run_trunk(x, mem_k, mem_v, conf_st, ssdc_st, ssd_st, ffn_st, prm):
    new = ([], [], [], [])
    for l in range(DEPTH):
        lp = {name: w[l] for name, w in prm.items()}
        x, st = decoder_layer(x, mem_k[l], mem_v[l], conf_st[l], ssdc_st[l], ssd_st[l], ffn_st[l], lp)
        for lst, s in zip(new, st):
            lst.append(s)
    return x, [jnp.stack(s) for s in new]


def setup_inputs(seed: int = 0) -> dict:
    key = jax.random.key(seed)
    ks = iter(jax.random.split(key, 48))
    f32 = jnp.float32

    def nrm(shape, scale):
        return jax.random.normal(next(ks), shape, f32) * scale

    def gain(shape):
        return 1.0 + nrm(shape, 0.02)

    L = DEPTH
    dt0 = jnp.exp(jax.random.uniform(next(ks), (L, SSD_HEADS), f32, math.log(1e-3), math.log(1e-1)))
    a_log = jnp.log(jax.random.uniform(next(ks), (L, SSD_HEADS), f32, 1.0, 16.0))
    return {
        "x_prompt": nrm((BATCH, SEQ, D_MODEL), 1.0),
        "x_sample": nrm((DEC_BATCH, DEC_SEQ, D_MODEL), 1.0),
        "cache_mem_k": nrm((L, DEC_BATCH, MEM_LEN, XA_HEADS, XA_HEAD_DIM), 1.0),
        "cache_mem_v": nrm((L, DEC_BATCH, MEM_LEN, XA_HEADS, XA_HEAD_DIM), BETA),
        "state_conf_conv": nrm((L, DEC_BATCH, CONF_KERNEL - 1, CONF_W), 0.5),
        "state_ssd_conv": nrm((L, DEC_BATCH, SSD_CONV - 1, XBC_W), 1.0),
        "state_ssd": nrm((L, DEC_BATCH, SSD_HEADS, SSD_HEAD_DIM, SSD_STATE), 0.3),
        "state_ffn_conv": nrm((L, DEC_BATCH, FFN_CONV - 1, 2 * D_FF), BETA),
        "mem_prompt": nrm((BATCH, MEM_LEN, D_MODEL), 1.0),
        "w_in": nrm((L, D_MODEL, D_IN), D_MODEL ** -0.5),
        "conf_conv_w": nrm((L, CONF_KERNEL, CONF_W), CONF_KERNEL ** -0.5),
        "conf_conv_b": nrm((L, CONF_W), 0.02),
        "conf_ln_g": gain((L, CONF_W)),
        "conf_ln_b": nrm((L, CONF_W), 0.02),
        "ssd_conv_w": nrm((L, SSD_CONV, XBC_W), SSD_CONV ** -0.5),
        "ssd_conv_b": nrm((L, XBC_W), 0.02),
        "ssd_dt_bias": dt0 + jnp.log(-jnp.expm1(-dt0)),
        "ssd_a_log": a_log,
        "ssd_d": gain((L, SSD_HEADS)),
        "ssd_norm_g": gain((L, SSD_W)),
        "w_out": nrm((L, D_MIX, D_MODEL), BETA * D_MIX ** -0.5),
        "ln_mix_g": gain((L, D_MODEL)),
        "ln_mix_b": nrm((L, D_MODEL), 0.02),
        "xa_wq": nrm((L, D_MODEL, XA_W), D_MODEL ** -0.5),
        "xa_wk": nrm((L, D_MODEL, XA_W), D_MODEL ** -0.5),
        "xa_wv": nrm((L, D_MODEL, XA_W), BETA * D_MODEL ** -0.5),
        "xa_wo": nrm((L, XA_W, D_MODEL), BETA * XA_W ** -0.5),
        "ln_xa_g": gain((L, D_MODEL)),
        "ln_xa_b": nrm((L, D_MODEL), 0.02),
        "ffn_w_up": nrm((L, D_MODEL, 2 * D_FF), BETA * D_MODEL ** -0.5),
        "ffn_conv_w": nrm((L, FFN_CONV, 2 * D_FF), FFN_CONV ** -0.5),
        "ffn_conv_b": nrm((L, 2 * D_FF), 0.02),
        "ffn_w_down": nrm((L, D_FF, D_MODEL), BETA * D_FF ** -0.5),
        "ln_ffn_g": gain((L, D_MODEL)),
        "ln_ffn_b": nrm((L, D_MODEL), 0.02),
    }


def reference(x_prompt, x_sample, cache_mem_k, cache_mem_v, state_conf_conv, state_ssd_conv, state_ssd,
              state_ffn_conv, mem_prompt, w_in, conf_conv_w, conf_conv_b, conf_ln_g, conf_ln_b,
              ssd_conv_w, ssd_conv_b, ssd_dt_bias, ssd_a_log, ssd_d, ssd_norm_g, w_out, ln_mix_g, ln_mix_b,
              xa_wq, xa_wk, xa_wv, xa_wo, ln_xa_g, ln_xa_b, ffn_w_up, ffn_conv_w, ffn_conv_b, ffn_w_down,
              ln_ffn_g, ln_ffn_b):
    prm = dict(w_in=w_in, conf_conv_w=conf_conv_w, conf_conv_b=conf_conv_b, conf_ln_g=conf_ln_g,
               conf_ln_b=conf_ln_b, ssd_conv_w=ssd_conv_w, ssd_conv_b=ssd_conv_b, ssd_dt_bias=ssd_dt_bias,
               ssd_a_log=ssd_a_log, ssd_d=ssd_d, ssd_norm_g=ssd_norm_g, w_out=w_out, ln_mix_g=ln_mix_g,
               ln_mix_b=ln_mix_b, xa_wq=xa_wq, xa_wo=xa_wo, ln_xa_g=ln_xa_g, ln_xa_b=ln_xa_b,
               ffn_w_up=ffn_w_up, ffn_conv_w=ffn_conv_w, ffn_conv_b=ffn_conv_b, ffn_w_down=ffn_w_down,
               ln_ffn_g=ln_ffn_g, ln_ffn_b=ln_ffn_b)
    bp = x_prompt.shape[0]
    dtp = x_prompt.dtype
    mk_p = jnp.einsum('bmd,lde->lbme', mem_prompt, xa_wk).reshape(DEPTH, bp, MEM_LEN, XA_HEADS, XA_HEAD_DIM)
    mv_p = jnp.einsum('bmd,lde->lbme', mem_prompt, xa_wv).reshape(DEPTH, bp, MEM_LEN, XA_HEADS, XA_HEAD_DIM)
    y_prompt, st_p = run_trunk(
        x_prompt, mk_p, mv_p,
        jnp.zeros((DEPTH, bp, CONF_KERNEL - 1, CONF_W), dtp),
        jnp.zeros((DEPTH, bp, SSD_CONV - 1, XBC_W), dtp),
        jnp.zeros((DEPTH, bp, SSD_HEADS, SSD_HEAD_DIM, SSD_STATE), dtp),
        jnp.zeros((DEPTH, bp, FFN_CONV - 1, 2 * D_FF), dtp),
        prm)
    y_sample, st_s = run_trunk(x_sample, cache_mem_k, cache_mem_v, state_conf_conv, state_ssd_conv,
                               state_ssd, state_ffn_conv, prm)
    return (y_prompt, y_sample, st_p[0], st_p[1], st_p[2], st_p[3], mk_p, mv_p,
            st_s[0], st_s[1], st_s[2], st_s[3])
```

```python
import functools

import jax
import jax.numpy as jnp
from jax import lax
from jax.experimental import pallas as pl
from jax.experimental.pallas import tpu as pltpu

F32 = jnp.float32
BF16 = jnp.bfloat16
HIGHEST = lax.Precision.HIGHEST

D_MODEL = 1024
DEPTH = 4
CONF_W = 1024
CONF_KERNEL = 31
SSD_HEAD_DIM = 64
SSD_HEADS = 16
SSD_W = 1024
SSD_GROUPS = 2
SSD_STATE = 128
SSD_CONV = 4
SSD_CHUNK = 128
XBC_W = SSD_W + 2 * SSD_GROUPS * SSD_STATE
MEM_LEN = 256
XA_HEADS = 4
XA_HEAD_DIM = 256
D_FF = 2816
FFN_CONV = 3
ALPHA = (2.0 * DEPTH) ** 0.25
LN_EPS = 1e-5

LANES = 128
C_GLU, C_Z, C_XBC, C_DT = 0, 2 * CONF_W, 2 * CONF_W + SSD_W, 2 * CONF_W + SSD_W + XBC_W
D_IN_PAD = C_DT + LANES
CONF_HALO = 32
SSD_HALO = 8
FFN_HALO = 8
FFN_COLS = 256
VMEM_LIMIT = 56 * 1024 * 1024


def _cparams(n_grid_dims):
    return pltpu.CompilerParams(dimension_semantics=("arbitrary",) * n_grid_dims,
                                vmem_limit_bytes=VMEM_LIMIT)


def _resident(shape):
    nd = len(shape)
    return pl.BlockSpec(shape, lambda *_: (0,) * nd, pipeline_mode=pl.Buffered(1))


def _sigmoid(x):
    return jax.nn.sigmoid(x)


def _silu(x):
    return x * _sigmoid(x)


def _softplus(x):
    return jnp.maximum(x, 0.0) + jnp.log(1.0 + jnp.exp(-jnp.abs(x)))


def _layer_norm(x, g, b):
    mu = jnp.mean(x, -1, keepdims=True)
    xc = x - mu
    var = jnp.mean(xc * xc, -1, keepdims=True)
    return xc * lax.rsqrt(var + LN_EPS) * g + b


def _dot(a, b):
    return jnp.dot(a, b, preferred_element_type=F32)


def _dot_f32(a, b):
    return jnp.dot(a, b, preferred_element_type=F32, precision=HIGHEST)


def _prompt_in_kernel(tt, x_ref, win_ref, cw_ref, cb_ref, cg_ref, cbeta_ref, sw_ref, sb_ref, dtb_ref,
                      a_ref, z_ref, xbc_ref, dt_ref, cst_ref, sst_ref, cbuf, sbuf):
    t = pl.program_id(1)

    @pl.when(t == 0)
    def _():
        cbuf[0:CONF_HALO, :] = jnp.zeros((CONF_HALO, CONF_W), F32)
        sbuf[0:SSD_HALO, :] = jnp.zeros((SSD_HALO, XBC_W), F32)

    xb = x_ref[0].astype(BF16)
    glu = _dot(xb, win_ref[:, C_GLU:C_GLU + 2 * CONF_W])
    cbuf[CONF_HALO:CONF_HALO + tt, :] = glu[:, :CONF_W] * _sigmoid(glu[:, CONF_W:])
    z_ref[0] = _dot(xb, win_ref[:, C_Z:C_Z + SSD_W])
    sbuf[SSD_HALO:SSD_HALO + tt, :] = _dot(xb, win_ref[:, C_XBC:C_XBC + XBC_W])
    dt_ref[0] = _softplus(_dot(xb, win_ref[:, C_DT:C_DT + LANES]) + dtb_ref[...])

    rc = 32
    off = CONF_HALO - (CONF_KERNEL - 1)
    for r in range(tt // rc):
        base = r * rc
        acc = cw_ref[0:1, :] * cbuf[base + off:base + off + rc, :]
        for k in range(1, CONF_KERNEL):
            acc = acc + cw_ref[k:k + 1, :] * cbuf[base + off + k:base + off + k + rc, :]
        y = _layer_norm(acc + cb_ref[...], cg_ref[...], cbeta_ref[...])
        a_ref[0, base:base + rc, :] = _silu(y).astype(BF16)
    cst_ref[0] = cbuf[tt + off:tt + CONF_HALO, :]
    cbuf[0:CONF_HALO, :] = cbuf[tt:tt + CONF_HALO, :]

    rc = 16
    off = SSD_HALO - (SSD_CONV - 1)
    for r in range(tt // rc):
        base = r * rc
        acc = sw_ref[0:1, :] * sbuf[base + off:base + off + rc, :]
        for k in range(1, SSD_CONV):
            acc = acc + sw_ref[k:k + 1, :] * sbuf[base + off + k:base + off + k + rc, :]
        xbc_ref[0, base:base + rc, :] = _silu(acc + sb_ref[...])
    sst_ref[0] = sbuf[tt + off:tt + SSD_HALO, :]
    sbuf[0:SSD_HALO, :] = sbuf[tt:tt + SSD_HALO, :]


def _prompt_in(x, win, cw, cb, cg, cbeta, sw, sb, dtb, tt):
    bsz, t, _ = x.shape
    grid = (bsz, t // tt)
    tile = lambda w: pl.BlockSpec((1, tt, w), lambda b, i: (b, i, 0))
    per_b = lambda r, w: pl.BlockSpec((1, r, w), lambda b, i: (b, 0, 0))
    return pl.pallas_call(
        functools.partial(_prompt_in_kernel, tt),
        grid=grid,
        in_specs=[tile(D_MODEL), _resident(win.shape), _resident(cw.shape), _resident(cb.shape),
                  _resident(cg.shape), _resident(cbeta.shape), _resident(sw.shape), _resident(sb.shape),
                  _resident(dtb.shape)],
        out_specs=[tile(CONF_W), tile(SSD_W), tile(XBC_W), tile(LANES),
                   per_b(CONF_KERNEL - 1, CONF_W), per_b(SSD_CONV - 1, XBC_W)],
        out_shape=[jax.ShapeDtypeStruct((bsz, t, CONF_W), BF16),
                   jax.ShapeDtypeStruct((bsz, t, SSD_W), F32),
                   jax.ShapeDtypeStruct((bsz, t, XBC_W), F32),
                   jax.ShapeDtypeStruct((bsz, t, LANES), F32),
                   jax.ShapeDtypeStruct((bsz, CONF_KERNEL - 1, CONF_W), F32),
                   jax.ShapeDtypeStruct((bsz, SSD_CONV - 1, XBC_W), F32)],
        scratch_shapes=[pltpu.VMEM((CONF_HALO + tt, CONF_W), F32),
                        pltpu.VMEM((SSD_HALO + tt, XBC_W), F32)],
        compiler_params=_cparams(2),
        name="prompt_in",
    )(x, win, cw, cb, cg, cbeta, sw, sb, dtb)


def _prompt_ssd_kernel(xbc_ref, dt_ref, z_ref, alog_ref, rexp_ref, dx_ref, y_ref, st_ref, ht):
    c = pl.program_id(1)
    q = SSD_CHUNK
    gw = SSD_W // SSD_GROUPS

    @pl.when(c == 0)
    def _():
        ht[...] = jnp.zeros(ht.shape, F32)

    xs = xbc_ref[0, :, 0:SSD_W]
    dt = dt_ref[0]
    rexp = rexp_ref[...]
    row = lax.broadcasted_iota(jnp.int32, (q, q), 0)
    col = lax.broadcasted_iota(jnp.int32, (q, q), 1)
    causal = row >= col
    a_neg = -jnp.exp(alog_ref[...])
    acum = _dot_f32(causal.astype(F32), dt * a_neg)
    acx = _dot_f32(acum, rexp)
    dtx = xs * _dot_f32(dt, rexp)
    act = acum.T
    a_last = acx[q - 1:q, :]
    dtxb = dtx.astype(BF16)
    wb = (dtx * jnp.exp(a_last - acx)).astype(BF16)
    decay_x = jnp.exp(acx)
    hb = ht[...].astype(BF16)
    lane = lax.broadcasted_iota(jnp.int32, (q, LANES), 1)
    first_head = lane < SSD_HEAD_DIM

    for g in range(SSD_GROUPS):
        bm = xbc_ref[0, :, SSD_W + g * SSD_STATE:SSD_W + (g + 1) * SSD_STATE]
        cm = xbc_ref[0, :, SSD_W + (SSD_GROUPS + g) * SSD_STATE:SSD_W + (SSD_GROUPS + g + 1) * SSD_STATE]
        bmb = bm.astype(BF16)
        cmb = cm.astype(BF16)
        cb = lax.dot_general(cmb, bmb, (((1,), (1,)), ((), ())), preferred_element_type=F32)
        y_off = _dot(cmb, hb[:, g * gw:(g + 1) * gw]) * decay_x[:, g * gw:(g + 1) * gw]
        for j in range(gw // LANES):
            h0 = (g * gw + j * LANES) // SSD_HEAD_DIM
            slab = dtxb[:, g * gw + j * LANES:g * gw + (j + 1) * LANES]
            yd = []
            for h in (h0, h0 + 1):
                seg = acum[:, h:h + 1] - act[h:h + 1, :]
                decay = jnp.exp(jnp.where(causal, seg, -1e30))
                yd.append(_dot((cb * decay).astype(BF16), slab))
            y_diag = jnp.where(first_head, yd[0], yd[1])
            lo = g * gw + j * LANES
            y = y_diag + y_off[:, j * LANES:(j + 1) * LANES] + dx_ref[:, lo:lo + LANES] * xs[:, lo:lo + LANES]
            zz = z_ref[0, :, lo:lo + LANES]
            y_ref[0, :, lo:lo + LANES] = (y * _silu(zz)).astype(y_ref.dtype)
        upd = _dot(bm.T.astype(BF16), wb[:, g * gw:(g + 1) * gw])
        ht[:, g * gw:(g + 1) * gw] = ht[:, g * gw:(g + 1) * gw] * jnp.exp(a_last[:, g * gw:(g + 1) * gw]) + upd

    @pl.when(c == pl.num_programs(1) - 1)
    def _():
        st_ref[0] = ht[...].T.reshape(SSD_HEADS, SSD_HEAD_DIM, SSD_STATE)


def _rmsnorm_kernel(y_ref, g_ref, o_ref):
    y = y_ref[...]
    ms = jnp.mean(y * y, -1, keepdims=True)
    o_ref[...] = (y * lax.rsqrt(ms + LN_EPS) * g_ref[...]).astype(o_ref.dtype)


def _prompt_ssd(xbc, dt, z, alog, rexp, dx, ng):
    bsz, t, _ = xbc.shape
    q = SSD_CHUNK
    grid = (bsz, t // q)
    tile = lambda w: pl.BlockSpec((1, q, w), lambda b, i: (b, i, 0))
    y, st = pl.pallas_call(
        _prompt_ssd_kernel,
        grid=grid,
        in_specs=[tile(XBC_W), tile(LANES), tile(SSD_W), _resident(alog.shape), _resident(rexp.shape),
                  _resident(dx.shape)],
        out_specs=[tile(SSD_W),
                   pl.BlockSpec((1, SSD_HEADS, SSD_HEAD_DIM, SSD_STATE), lambda b, i: (b, 0, 0, 0))],
        out_shape=[jax.ShapeDtypeStruct((bsz, t, SSD_W), F32),
                   jax.ShapeDtypeStruct((bsz, SSD_HEADS, SSD_HEAD_DIM, SSD_STATE), F32)],
        scratch_shapes=[pltpu.VMEM((SSD_STATE, SSD_W), F32)],
        compiler_params=_cparams(2),
        name="prompt_ssd",
    )(xbc, dt, z, alog, rexp, dx)
    m = bsz * t
    tm = 512
    yn = pl.pallas_call(
        _rmsnorm_kernel,
        grid=(m // tm,),
        in_specs=[pl.BlockSpec((tm, SSD_W), lambda i: (i, 0)), _resident(ng.shape)],
        out_specs=pl.BlockSpec((tm, SSD_W), lambda i: (i, 0)),
        out_shape=jax.ShapeDtypeStruct((m, SSD_W), BF16),
        compiler_params=_cparams(1),
        name="prompt_ssd_norm",
    )(y.reshape(m, SSD_W), ng)
    return yn, st


def _proj_ln_kernel(n_in, *refs):
    in_refs, w_refs = refs[:n_in], refs[n_in:2 * n_in]
    x_ref, g_ref, b_ref, o_ref = refs[2 * n_in:]
    acc = ALPHA * x_ref[...]
    for a_ref, w_ref in zip(in_refs, w_refs):
        acc = acc + _dot(a_ref[...].astype(BF16), w_ref[...])
    o_ref[...] = _layer_norm(acc, g_ref[...], b_ref[...])


def _proj_ln(ins, ws, resid, g, b, tm):
    m, d = resid.shape
    n_in = len(ins)
    rows = lambda a: pl.BlockSpec((tm, a.shape[1]), lambda i: (i, 0))
    return pl.pallas_call(
        functools.partial(_proj_ln_kernel, n_in),
        grid=(m // tm,),
        in_specs=[rows(a) for a in ins] + [_resident(w.shape) for w in ws]
        + [rows(resid), _resident(g.shape), _resident(b.shape)],
        out_specs=rows(resid),
        out_shape=jax.ShapeDtypeStruct((m, d), F32),
        compiler_params=_cparams(1),
        name="proj_ln",
    )(*ins, *ws, resid, g, b)


def _prompt_xattn_kernel(x_ref, k_ref, v_ref, wq_ref, wo_ref, g_ref, b_ref, o_ref):
    x = x_ref[0]
    q = _dot(x.astype(BF16), wq_ref[...])
    acc = ALPHA * x
    for hd in range(XA_HEADS):
        sl = slice(hd * XA_HEAD_DIM, (hd + 1) * XA_HEAD_DIM)
        qh = q[:, sl].astype(BF16)
        kh = k_ref[0, :, sl].astype(BF16)
        s = lax.dot_general(qh, kh, (((1,), (1,)), ((), ())), preferred_element_type=F32)
        s = s * (XA_HEAD_DIM ** -0.5)
        e = jnp.exp(s - jnp.max(s, -1, keepdims=True))
        p = e / jnp.sum(e, -1, keepdims=True)
        oh = _dot(p.astype(BF16), v_ref[0, :, sl].astype(BF16))
        acc = acc + _dot(oh.astype(BF16), wo_ref[sl, :])
    o_ref[0] = _layer_norm(acc, g_ref[...], b_ref[...])


def _prompt_xattn(x, mk, mv, layer, wq, wo, g, b, tt):
    bsz, t, d = x.shape
    tile = pl.BlockSpec((1, tt, d), lambda bi, i: (bi, i, 0))
    mem = pl.BlockSpec((None, 1, MEM_LEN, d), lambda bi, i: (layer, bi, 0, 0))
    return pl.pallas_call(
        _prompt_xattn_kernel,
        grid=(bsz, t // tt),
        in_specs=[tile, mem, mem, _resident(wq.shape), _resident(wo.shape), _resident(g.shape),
                  _resident(b.shape)],
        out_specs=tile,
        out_shape=jax.ShapeDtypeStruct((bsz, t, d), F32),
        compiler_params=_cparams(2),
        name="prompt_xattn",
    )(x, mk, mv, wq, wo, g, b)


def _prompt_ffn_kernel(tt, x_ref, wup_ref, wdn_ref, cw_ref, cb_ref, g_ref, b_ref, o_ref, st_ref, hst, ubuf):
    t = pl.program_id(1)

    @pl.when(t == 0)
    def _():
        hst[...] = jnp.zeros(hst.shape, F32)

    x = x_ref[0]
    xb = x.astype(BF16)
    acc = ALPHA * x
    off = FFN_HALO - (FFN_CONV - 1)
    for j in range(D_FF // FFN_COLS):
        conv = []
        for half in range(2):
            c0 = half * D_FF + j * FFN_COLS
            cs = slice(c0, c0 + FFN_COLS)
            u = _dot(xb, wup_ref[:, cs])
            ubuf[half, 0:FFN_HALO, :] = hst[:, cs]
            ubuf[half, FFN_HALO:FFN_HALO + tt, :] = u
            cv = cb_ref[:, cs] + cw_ref[FFN_CONV - 1:FFN_CONV, cs] * u
            for k in range(FFN_CONV - 1):
                cv = cv + cw_ref[k:k + 1, cs] * ubuf[half, off + k:off + k + tt, :]
            hst[:, cs] = ubuf[half, tt:tt + FFN_HALO, :]
            conv.append(cv)
        f = _silu(conv[1]) * conv[0]
        acc = acc + _dot(f.astype(BF16), wdn_ref[j * FFN_COLS:(j + 1) * FFN_COLS, :])
    o_ref[0] = _layer_norm(acc, g_ref[...], b_ref[...])
    st_ref[0] = hst[off:FFN_HALO, :]


def _prompt_ffn(x, wup, wdn, cw, cb, g, b, tt):
    bsz, t, d = x.shape
    tile = pl.BlockSpec((1, tt, d), lambda bi, i: (bi, i, 0))
    return pl.pallas_call(
        functools.partial(_prompt_ffn_kernel, tt),
        grid=(bsz, t // tt),
        in_specs=[tile, _resident(wup.shape), _resident(wdn.shape), _resident(cw.shape),
                  _resident(cb.shape), _resident(g.shape), _resident(b.shape)],
        out_specs=[tile, pl.BlockSpec((1, FFN_CONV - 1, 2 * D_FF), lambda bi, i: (bi, 0, 0))],
        out_shape=[jax.ShapeDtypeStruct((bsz, t, d), F32),
                   jax.ShapeDtypeStruct((bsz, FFN_CONV - 1, 2 * D_FF), F32)],
        scratch_shapes=[pltpu.VMEM((FFN_HALO, 2 * D_FF), F32),
                        pltpu.VMEM((2, FFN_HALO + tt, FFN_COLS), F32)],
        compiler_params=_cparams(2),
        name="prompt_ffn",
    )(x, wup, wdn, cw, cb, g, b)


def _mem_proj_kernel(x_ref, w_ref, o_ref):
    o_ref[...] = _dot(x_ref[...].astype(BF16), w_ref[...])


def _mem_proj(mem, w, tm):
    m, d = mem.shape
    nl, _, n = w.shape
    return pl.pallas_call(
        _mem_proj_kernel,
        grid=(nl, m // tm),
        in_specs=[pl.BlockSpec((tm, d), lambda l, i: (i, 0)),
                  pl.BlockSpec((None, d, n), lambda l, i: (l, 0, 0))],
        out_specs=pl.BlockSpec((None, tm, n), lambda l, i: (l, i, 0)),
        out_shape=jax.ShapeDtypeStruct((nl, m, n), F32),
        compiler_params=_cparams(2),
        name="mem_proj",
    )(mem, w)


def _mm_k_kernel(x_ref, w_ref, o_ref):
    @pl.when(pl.program_id(0) == 0)
    def _():
        o_ref[...] = jnp.zeros(o_ref.shape, F32)

    o_ref[...] += _dot(x_ref[...].astype(BF16), w_ref[...])


def _mm_k(x, w, tk=256):
    m, k = x.shape
    n = w.shape[1]
    return pl.pallas_call(
        _mm_k_kernel,
        grid=(k // tk,),
        in_specs=[pl.BlockSpec((m, tk), lambda i: (0, i)), pl.BlockSpec((tk, n), lambda i: (i, 0))],
        out_specs=pl.BlockSpec((m, n), lambda i: (0, 0)),
        out_shape=jax.ShapeDtypeStruct((m, n), F32),
        compiler_params=_cparams(1),
        name="mm_k",
    )(x, w)


def _sample_conf_kernel(h_ref, st_ref, cw_ref, cb_ref, g_ref, b_ref, a_ref, nst_ref):
    glu = h_ref[...]
    a = glu[:, :CONF_W] * _sigmoid(glu[:, CONF_W:])
    nk = CONF_KERNEL - 1
    acc = cb_ref[...] + cw_ref[nk:nk + 1, :] * a
    for k in range(nk):
        acc = acc + cw_ref[k:k + 1, :] * st_ref[:, k * CONF_W:(k + 1) * CONF_W]
    a_ref[...] = _silu(_layer_norm(acc, g_ref[...], b_ref[...])).astype(BF16)
    nst_ref[:, 0:(nk - 1) * CONF_W] = st_ref[:, CONF_W:nk * CONF_W]
    nst_ref[:, (nk - 1) * CONF_W:nk * CONF_W] = a


def _sample_conf(h, st, layer, cw, cb, g, b, bb=32):
    n = h.shape[0]
    sw = (CONF_KERNEL - 1) * CONF_W
    return pl.pallas_call(
        _sample_conf_kernel,
        grid=(n // bb,),
        in_specs=[pl.BlockSpec((bb, 2 * CONF_W), lambda i: (i, 0)),
                  pl.BlockSpec((None, bb, sw), lambda i: (layer, i, 0)),
                  _resident(cw.shape), _resident(cb.shape), _resident(g.shape), _resident(b.shape)],
        out_specs=[pl.BlockSpec((bb, CONF_W), lambda i: (i, 0)), pl.BlockSpec((bb, sw), lambda i: (i, 0))],
        out_shape=[jax.ShapeDtypeStruct((n, CONF_W), BF16), jax.ShapeDtypeStruct((n, sw), F32)],
        compiler_params=_cparams(1),
        name="sample_conf",
    )(h, st, cw, cb, g, b)


def _sample_ssd_prep_kernel(xbc_ref, dt_ref, st_ref, sw_ref, sb_ref, dtb_ref, alog_ref, rexp_ref,
                            nst_ref, xs_ref, bc_ref, xdt_t_ref, decay_t_ref):
    xr = xbc_ref[...]
    nk = SSD_CONV - 1
    acc = sb_ref[...] + sw_ref[nk:nk + 1, :] * xr
    for k in range(nk):
        acc = acc + sw_ref[k:k + 1, :] * st_ref[:, k * XBC_W:(k + 1) * XBC_W]
    xbc = _silu(acc)
    nst_ref[:, 0:(nk - 1) * XBC_W] = st_ref[:, XBC_W:nk * XBC_W]
    nst_ref[:, (nk - 1) * XBC_W:nk * XBC_W] = xr
    xs = xbc[:, :SSD_W]
    xs_ref[...] = xs
    bc_ref[...] = xbc[:, SSD_W:]
    dt = _softplus(dt_ref[...] + dtb_ref[...])
    rexp = rexp_ref[...]
    a_neg = -jnp.exp(alog_ref[...])
    xdt_t_ref[...] = (xs * _dot_f32(dt, rexp)).T
    decay_t_ref[...] = jnp.exp(_dot_f32(dt * a_neg, rexp)).T


def _sample_ssd_prep(h, st, layer, sw, sb, dtb, alog, rexp):
    n = h.shape[0]
    stw = (SSD_CONV - 1) * XBC_W
    full = lambda r, c: pl.BlockSpec((r, c), lambda i: (0, 0))
    return pl.pallas_call(
        _sample_ssd_prep_kernel,
        grid=(1,),
        in_specs=[pl.BlockSpec((n, XBC_W), lambda i: (0, C_XBC // XBC_W)),
                  pl.BlockSpec((n, LANES), lambda i: (0, C_DT // LANES)),
                  pl.BlockSpec((None, n, stw), lambda i: (layer, 0, 0)),
                  full(*sw.shape), full(*sb.shape), full(*dtb.shape), full(*alog.shape), full(*rexp.shape)],
        out_specs=[full(n, stw), full(n, SSD_W), full(n, XBC_W - SSD_W), full(SSD_W, n), full(SSD_W, n)],
        out_shape=[jax.ShapeDtypeStruct((n, stw), F32), jax.ShapeDtypeStruct((n, SSD_W), F32),
                   jax.ShapeDtypeStruct((n, XBC_W - SSD_W), F32), jax.ShapeDtypeStruct((SSD_W, n), F32),
                   jax.ShapeDtypeStruct((SSD_W, n), F32)],
        compiler_params=_cparams(1),
        name="sample_ssd_prep",
    )(h, h, st, sw, sb, dtb, alog, rexp)


def _sample_ssd_state_kernel(n, st_ref, xdt_t_ref, decay_t_ref, b_ref, c_ref, nst_ref, y_t_ref):
    for b in range(n):
        hn = st_ref[b] * decay_t_ref[:, b:b + 1] + xdt_t_ref[:, b:b + 1] * b_ref[b:b + 1, :]
        nst_ref[b] = hn
        y_t_ref[:, b:b + 1] = jnp.sum(hn * c_ref[b:b + 1, :], axis=-1, keepdims=True)


def _sample_ssd_state(st, layer, xdt_t, decay_t, bc):
    n = st.shape[1]
    hpg = SSD_HEADS // SSD_GROUPS
    col = pl.BlockSpec((SSD_HEAD_DIM, n), lambda h: (h, 0))
    return pl.pallas_call(
        functools.partial(_sample_ssd_state_kernel, n),
        grid=(SSD_HEADS,),
        in_specs=[pl.BlockSpec((None, n, None, SSD_HEAD_DIM, SSD_STATE), lambda h: (layer, 0, h, 0, 0)),
                  col, col,
                  pl.BlockSpec((n, SSD_STATE), lambda h: (0, h // hpg)),
                  pl.BlockSpec((n, SSD_STATE), lambda h: (0, SSD_GROUPS + h // hpg))],
        out_specs=[pl.BlockSpec((n, None, SSD_HEAD_DIM, SSD_STATE), lambda h: (0, h, 0, 0)), col],
        out_shape=[jax.ShapeDtypeStruct((n, SSD_HEADS, SSD_HEAD_DIM, SSD_STATE), F32),
                   jax.ShapeDtypeStruct((SSD_W, n), F32)],
        compiler_params=_cparams(1),
        name="sample_ssd_state",
    )(st, xdt_t, decay_t, bc, bc)


def _sample_mix_out_kernel(y_t_ref, xs_ref, z_ref, a_ref, x_ref, wo_ref, dx_ref, ng_ref, g_ref, b_ref, o_ref):
    y = y_t_ref[...].T + dx_ref[...] * xs_ref[...]
    y = y * _silu(z_ref[...])
    ms = jnp.mean(y * y, -1, keepdims=True)
    y = y * lax.rsqrt(ms + LN_EPS) * ng_ref[...]
    acc = ALPHA * x_ref[...] + _dot(a_ref[...], wo_ref[0:CONF_W, :]) + _dot(y.astype(BF16), wo_ref[CONF_W:, :])
    o_ref[...] = _layer_norm(acc, g_ref[...], b_ref[...])


def _sample_mix_out(y_t, xs, h, a, x, wo, dx, ng, g, b):
    n = x.shape[0]
    full = lambda arr: pl.BlockSpec(arr.shape, lambda i: (0,) * arr.ndim)
    return pl.pallas_call(
        _sample_mix_out_kernel,
        grid=(1,),
        in_specs=[full(y_t), full(xs), pl.BlockSpec((n, SSD_W), lambda i: (0, C_Z // SSD_W)), full(a),
                  full(x), full(wo), full(dx), full(ng), full(g), full(b)],
        out_specs=full(x),
        out_shape=jax.ShapeDtypeStruct(x.shape, F32),
        compiler_params=_cparams(1),
        name="sample_mix_out",
    )(y_t, xs, h, a, x, wo, dx, ng, g, b)


def _sample_attn_kernel(bb, q_ref, k_ref, v_ref, o_ref):
    for j in range(bb):
        prod = k_ref[j] * q_ref[j]
        outs = []
        for hd in range(XA_HEADS):
            sl = slice(hd * XA_HEAD_DIM, (hd + 1) * XA_HEAD_DIM)
            s = jnp.sum(prod[:, sl], axis=-1, keepdims=True) * (XA_HEAD_DIM ** -0.5)
            e = jnp.exp(s - jnp.max(s, axis=0, keepdims=True))
            p = e / jnp.sum(e, axis=0, keepdims=True)
            outs.append(jnp.sum(p * v_ref[j, :, sl], axis=0, keepdims=True))
        o_ref[j] = jnp.concatenate(outs, axis=-1)


def _sample_attn(q, mk, mv, layer, bb=4):
    n, d = q.shape
    row = pl.BlockSpec((bb, 1, d), lambda i: (i, 0, 0))
    mem = pl.BlockSpec((None, bb, MEM_LEN, d), lambda i: (layer, i, 0, 0))
    o = pl.pallas_call(
        functools.partial(_sample_attn_kernel, bb),
        grid=(n // bb,),
        in_specs=[row, mem, mem],
        out_specs=row,
        out_shape=jax.ShapeDtypeStruct((n, 1, d), F32),
        compiler_params=_cparams(1),
        name="sample_attn",
    )(q.reshape(n, 1, d), mk, mv)
    return o.reshape(n, d)


def _sample_ffn_kernel(u_ref, st_ref, cw_ref, cb_ref, x_ref, wdn_ref, g_ref, b_ref, o_ref, nst_ref):
    w2 = 2 * D_FF
    u = u_ref[...]
    cv = cb_ref[...] + cw_ref[0:1, :] * st_ref[:, 0:w2] + cw_ref[1:2, :] * st_ref[:, w2:2 * w2] \
        + cw_ref[2:3, :] * u
    nst_ref[:, 0:w2] = st_ref[:, w2:2 * w2]
    nst_ref[:, w2:2 * w2] = u
    f = _silu(cv[:, D_FF:]) * cv[:, :D_FF]
    acc = ALPHA * x_ref[...] + _dot(f.astype(BF16), wdn_ref[...])
    o_ref[...] = _layer_norm(acc, g_ref[...], b_ref[...])


def _sample_ffn(u, st, layer, cw, cb, x, wdn, g, b):
    n = x.shape[0]
    stw = (FFN_CONV - 1) * 2 * D_FF
    one = lambda arr: pl.BlockSpec(arr.shape, lambda i: (0,) * arr.ndim, pipeline_mode=pl.Buffered(1))
    return pl.pallas_call(
        _sample_ffn_kernel,
        grid=(1,),
        in_specs=[one(u), pl.BlockSpec((None, n, stw), lambda i: (layer, 0, 0), pipeline_mode=pl.Buffered(1)),
                  one(cw), one(cb), one(x), one(wdn), one(g), one(b)],
        out_specs=[pl.BlockSpec(x.shape, lambda i: (0, 0)), pl.BlockSpec((n, stw), lambda i: (0, 0))],
        out_shape=[jax.ShapeDtypeStruct(x.shape, F32), jax.ShapeDtypeStruct((n, stw), F32)],
        compiler_params=_cparams(1),
        name="sample_ffn",
    )(u, st, cw, cb, x, wdn, g, b)


def kernel(x_prompt, x_sample, cache_mem_k, cache_mem_v, state_conf_conv, state_ssd_conv, state_ssd,
           state_ffn_conv, mem_prompt, w_in, conf_conv_w, conf_conv_b, conf_ln_g, conf_ln_b, ssd_conv_w,
           ssd_conv_b, ssd_dt_bias, ssd_a_log, ssd_d, ssd_norm_g, w_out, ln_mix_g, ln_mix_b, xa_wq, xa_wk,
           xa_wv, xa_wo, ln_xa_g, ln_xa_b, ffn_w_up, ffn_conv_w, ffn_conv_b, ffn_w_down, ln_ffn_g, ln_ffn_b):
    bp, t, d = x_prompt.shape
    ns = x_sample.shape[0]
    nl = DEPTH
    tt = min(256, t)

    row = lambda p, l: p[l][None, :]
    pad_heads = lambda p, l: jnp.pad(p[l], (0, LANES - SSD_HEADS))[None, :]
    rexp = jnp.pad(jnp.repeat(jnp.eye(SSD_HEADS, dtype=F32), SSD_HEAD_DIM, axis=1),
                   ((0, LANES - SSD_HEADS), (0, 0)))

    mem2d = mem_prompt.reshape(bp * MEM_LEN, d)
    mk_p = _mem_proj(mem2d, xa_wk.astype(BF16), 512).reshape(nl, bp, MEM_LEN, d)
    mv_p = _mem_proj(mem2d, xa_wv.astype(BF16), 512).reshape(nl, bp, MEM_LEN, d)
    mk_s = cache_mem_k.reshape(nl, ns, MEM_LEN, d)
    mv_s = cache_mem_v.reshape(nl, ns, MEM_LEN, d)
    conf_s = state_conf_conv.reshape(nl, ns, (CONF_KERNEL - 1) * CONF_W)
    ssdc_s = state_ssd_conv.reshape(nl, ns, (SSD_CONV - 1) * XBC_W)
    ffn_s = state_ffn_conv.reshape(nl, ns, (FFN_CONV - 1) * 2 * D_FF)

    xp = x_prompt
    xs_ = x_sample.reshape(ns, d)
    st_p = ([], [], [], [])
    st_s = ([], [], [], [])
    for l in range(nl):
        win = jnp.pad(w_in[l], ((0, 0), (0, D_IN_PAD - w_in.shape[2]))).astype(BF16)
        wout = w_out[l].astype(BF16)
        wq, wo = xa_wq[l].astype(BF16), xa_wo[l].astype(BF16)
        wup, wdn = ffn_w_up[l].astype(BF16), ffn_w_down[l].astype(BF16)
        cw, cb = conf_conv_w[l], row(conf_conv_b, l)
        cg, cbeta = row(conf_ln_g, l), row(conf_ln_b, l)
        sw, sb = ssd_conv_w[l], row(ssd_conv_b, l)
        dtb, alog = pad_heads(ssd_dt_bias, l), pad_heads(ssd_a_log, l)
        dx = jnp.repeat(ssd_d[l], SSD_HEAD_DIM)[None, :]
        ng = row(ssd_norm_g, l)
        g1, b1 = row(ln_mix_g, l), row(ln_mix_b, l)
        g2, b2 = row(ln_xa_g, l), row(ln_xa_b, l)
        g3, b3 = row(ln_ffn_g, l), row(ln_ffn_b, l)
        fcw, fcb = ffn_conv_w[l], row(ffn_conv_b, l)

        a, z, xbc, dt, cst, sst = _prompt_in(xp, win, cw, cb, cg, cbeta, sw, sb, dtb, tt)
        yn, hst = _prompt_ssd(xbc, dt, z, alog, rexp, dx, ng)
        x2d = xp.reshape(bp * t, d)
        x1 = _proj_ln([a.reshape(bp * t, CONF_W), yn], [wout[:CONF_W], wout[CONF_W:]], x2d, g1, b1, 512)
        x2 = _prompt_xattn(x1.reshape(bp, t, d), mk_p, mv_p, l, wq, wo, g2, b2, tt)
        xp, fst = _prompt_ffn(x2, wup, wdn, fcw, fcb, g3, b3, tt)
        for lst, s in zip(st_p, (cst, sst, hst, fst)):
            lst.append(s)

        h = _mm_k(xs_, win)
        a_s, ncs = _sample_conf(h, conf_s, l, cw, cb, cg, cbeta)
        nss, xs1, bc, xdt_t, decay_t = _sample_ssd_prep(h, ssdc_s, l, sw, sb, dtb, alog, rexp)
        nst, y_t = _sample_ssd_state(state_ssd, l, xdt_t, decay_t, bc)
        x1s = _sample_mix_out(y_t, xs1, h, a_s, xs_, wout, dx, ng, g1, b1)
        qs = _mm_k(x1s, wq)
        os_ = _sample_attn(qs, mk_s, mv_s, l)
        x2s = _proj_ln([os_], [wo], x1s, g2, b2, ns)
        us = _mm_k(x2s, wup)
        xs_, nfs = _sample_ffn(us, ffn_s, l, fcw, fcb, x2s, wdn, g3, b3)
        for lst, s in zip(st_s, (ncs, nss, nst, nfs)):
            lst.append(s)

    st_p = [jnp.stack(s) for s in st_p]
    st_s = [jnp.stack(s) for s in st_s]
    hd = (XA_HEADS, XA_HEAD_DIM)
    return (xp, xs_.reshape(ns, 1, d), st_p[0], st_p[1], st_p[2], st_p[3],
            mk_p.reshape(nl, bp, MEM_LEN, *hd), mv_p.reshape(nl, bp, MEM_LEN, *hd),
            st_s[0].reshape(nl, ns, CONF_KERNEL - 1, CONF_W), st_s[1].reshape(nl, ns, SSD_CONV - 1, XBC_W),
            st_s[2], st_s[3].reshape(nl, ns, FFN_CONV - 1, 2 * D_FF))
```

```python
import functools

import jax
import jax.numpy as jnp
from jax import lax
from jax.experimental import pallas as pl
from jax.experimental.pallas import tpu as pltpu

F32 = jnp.float32
BF16 = jnp.bfloat16
HIGHEST = lax.Precision.HIGHEST

D_MODEL = 1024
DEPTH = 4
CONF_W = 1024
CONF_KERNEL = 31
SSD_HEAD_DIM = 64
SSD_HEADS = 16
SSD_W = 1024
SSD_GROUPS = 2
SSD_STATE = 128
SSD_CONV = 4
SSD_CHUNK = 128
XBC_W = SSD_W + 2 * SSD_GROUPS * SSD_STATE
MEM_LEN = 256
XA_HEADS = 4
XA_HEAD_DIM = 256
D_FF = 2816
FFN_CONV = 3
ALPHA = (2.0 * DEPTH) ** 0.25
LN_EPS = 1e-5

LANES = 128
C_GLU, C_Z, C_XBC, C_DT = 0, 2 * CONF_W, 2 * CONF_W + SSD_W, 2 * CONF_W + SSD_W + XBC_W
D_IN_PAD = C_DT + LANES
MIXER_TILE = 256
ROW_TILE = 512
FFN_COLS = 256
VMEM_LIMIT = 56 * 1024 * 1024


def _cparams(n_grid_dims):
    return pltpu.CompilerParams(dimension_semantics=("arbitrary",) * n_grid_dims,
                                vmem_limit_bytes=VMEM_LIMIT)


def _resident(shape):
    nd = len(shape)
    return pl.BlockSpec(shape, lambda *_: (0,) * nd, pipeline_mode=pl.Buffered(1))


def _sigmoid(x):
    return jax.nn.sigmoid(x)


def _silu(x):
    return x * _sigmoid(x)


def _softplus(x):
    return jnp.maximum(x, 0.0) + jnp.log(1.0 + jnp.exp(-jnp.abs(x)))


def _layer_norm(x, g, b):
    mu = jnp.mean(x, -1, keepdims=True)
    xc = x - mu
    var = jnp.mean(xc * xc, -1, keepdims=True)
    return xc * lax.rsqrt(var + LN_EPS) * g + b


def _dot(a, b):
    return jnp.dot(a, b, preferred_element_type=F32)


def _dot_f32(a, b):
    return jnp.dot(a, b, preferred_element_type=F32, precision=HIGHEST)


CHUNK = SSD_CHUNK
SLABS = CHUNK // 8


def _permuted_row(token):
    return 8 * (token % SLABS) + token // SLABS


def _wrapped_slab(cur, prev, shift, sub):
    return jnp.where(sub >= shift, pltpu.roll(cur, shift, 0), pltpu.roll(prev, shift, 0))


def _fill_history(src, base, eseq, nk, width):
    sub = lax.broadcasted_iota(jnp.int32, (8, width), 0)
    for i in range(SLABS):
        for shift in (1, 2):
            p = i - SLABS * shift + (nk - 1)
            if 0 <= p < nk - 1:
                cur = src[base + 8 * i:base + 8 * (i + 1), :]
                prev = src[base - CHUNK + 8 * i:base - CHUNK + 8 * (i + 1), :]
                eseq[8 * p:8 * (p + 1), :] = _wrapped_slab(cur, prev, shift, sub)
    eseq[8 * (nk - 1):8 * (nk - 1) + CHUNK, :] = src[base:base + CHUNK, :]


def _conv_from_history(eseq, w_ref, nk, lo, rows):
    r0, nr = rows
    acc = w_ref[0:1, lo:lo + LANES] * eseq[r0:r0 + nr, lo:lo + LANES]
    for k in range(1, nk):
        acc = acc + w_ref[k:k + 1, lo:lo + LANES] * eseq[8 * k + r0:8 * k + r0 + nr, lo:lo + LANES]
    return acc


def _permute_rows_kernel(tt, inverse, *refs):
    in_refs, o_ref = refs[:-1], refs[-1]
    for c in range(tt // CHUNK):
        for lt, x_ref in enumerate(in_refs):
            ls = slice(lt * LANES, (lt + 1) * LANES)
            if not inverse:
                for i in range(SLABS):
                    o_ref[0, c * CHUNK + 8 * i:c * CHUNK + 8 * (i + 1), ls] = \
                        x_ref[0, pl.ds(c * CHUNK + i, 8, stride=SLABS), :]
            else:
                for s in range(8):
                    for i0 in range(0, SLABS, 8):
                        tok = c * CHUNK + s * SLABS + i0
                        o_ref[0, tok:tok + 8, ls] = x_ref[0, pl.ds(c * CHUNK + 8 * i0 + s, 8, stride=8), :]


def _permute_rows(x, inverse, tt=512):
    bsz, t, w = x.shape
    tt = min(tt, t)
    lane_tile = lambda lt: pl.BlockSpec((1, tt, LANES), lambda bi, i: (bi, i, lt))
    return pl.pallas_call(
        functools.partial(_permute_rows_kernel, tt, inverse),
        grid=(bsz, t // tt),
        in_specs=[lane_tile(lt) for lt in range(w // LANES)],
        out_specs=pl.BlockSpec((1, tt, w), lambda bi, i: (bi, i, 0)),
        out_shape=jax.ShapeDtypeStruct(x.shape, x.dtype),
        compiler_params=_cparams(2),
        name="permute_rows",
    )(*([x] * (w // LANES)))


def _permuted_time(n):
    r = lax.broadcasted_iota(jnp.int32, (CHUNK, CHUNK), n)
    return (r & 7) * SLABS + (r >> 3)


def _prompt_mixer_kernel(tt, x_ref, win_ref, wout_ref, cw_ref, cb_ref, cg_ref, cbeta_ref, sw_ref, sb_ref,
                         dtb_ref, alog_ref, dx_ref, ng_ref, g_ref, b_ref,
                         o_ref, cst_ref, sst_ref, hst_ref,
                         abuf, xbuf, zbuf, dtbuf, eseq, eseq2, cvbuf, xact, ygbuf, abf, ybf, ht):
    t = pl.program_id(1)
    nch = tt // CHUNK

    @pl.when(t == 0)
    def _():
        abuf[0:CHUNK, :] = jnp.zeros((CHUNK, CONF_W), F32)
        xbuf[0:CHUNK, :] = jnp.zeros((CHUNK, XBC_W), F32)
        ht[...] = jnp.zeros(ht.shape, F32)

    xb = x_ref[0].astype(BF16)
    glu = _dot(xb, win_ref[:, C_GLU:C_GLU + 2 * CONF_W])
    abuf[CHUNK:CHUNK + tt, :] = glu[:, :CONF_W] * _sigmoid(glu[:, CONF_W:])
    zbuf[...] = _dot(xb, win_ref[:, C_Z:C_Z + SSD_W])
    xbuf[CHUNK:CHUNK + tt, :] = _dot(xb, win_ref[:, C_XBC:C_XBC + XBC_W])
    dtbuf[...] = _softplus(_dot(xb, win_ref[:, C_DT:C_DT + LANES]) + dtb_ref[...])

    tl, ts = _permuted_time(0), _permuted_time(1)
    causal = tl >= ts
    causal_f = causal.astype(F32)
    a_neg = -jnp.exp(alog_ref[...])
    first_head = lax.broadcasted_iota(jnp.int32, (CHUNK, LANES), 1) < SSD_HEAD_DIM
    rb = 32

    for c in range(nch):
        base = CHUNK * (c + 1)
        r0 = CHUNK * c
        _fill_history(abuf, base, eseq, CONF_KERNEL, CONF_W)
        for lt in range(CONF_W // LANES):
            for r in range(CHUNK // rb):
                cvbuf[r * rb:(r + 1) * rb, lt * LANES:(lt + 1) * LANES] = _conv_from_history(
                    eseq, cw_ref, CONF_KERNEL, lt * LANES, (r * rb, rb))
        y = _layer_norm(cvbuf[...] + cb_ref[...], cg_ref[...], cbeta_ref[...])
        abf[r0:r0 + CHUNK, :] = _silu(y).astype(BF16)

        _fill_history(xbuf, base, eseq2, SSD_CONV, XBC_W)
        for lt in range(XBC_W // LANES):
            lo = lt * LANES
            acc = _conv_from_history(eseq2, sw_ref, SSD_CONV, lo, (0, CHUNK))
            xact[:, lo:lo + LANES] = _silu(acc + sb_ref[:, lo:lo + LANES])

        dt = dtbuf[r0:r0 + CHUNK, :]
        acum = _dot_f32(causal_f, dt * a_neg)
        act = acum.T
        ss = jnp.zeros((CHUNK, 1), F32)
        for g in range(SSD_GROUPS):
            bm = xact[:, SSD_W + g * SSD_STATE:SSD_W + (g + 1) * SSD_STATE]
            cmb = xact[:, SSD_W + (SSD_GROUPS + g) * SSD_STATE:SSD_W + (SSD_GROUPS + g + 1) * SSD_STATE].astype(BF16)
            cb = lax.dot_general(cmb, bm.astype(BF16), (((1,), (1,)), ((), ())), preferred_element_type=F32)
            bmt = bm.T.astype(BF16)
            for jj in range(SSD_W // SSD_GROUPS // LANES):
                lo = (g * (SSD_W // SSD_GROUPS // LANES) + jj) * LANES
                h0 = lo // SSD_HEAD_DIM
                dtl = jnp.where(first_head, dt[:, h0:h0 + 1], dt[:, h0 + 1:h0 + 2])
                acl = jnp.where(first_head, acum[:, h0:h0 + 1], acum[:, h0 + 1:h0 + 2])
                xs = xact[:, lo:lo + LANES]
                dtx = xs * dtl
                dtxb = dtx.astype(BF16)
                a_last = acl[CHUNK - 1:CHUNK, :]
                hprev = ht[:, lo:lo + LANES]
                y_off = _dot(cmb, hprev.astype(BF16)) * jnp.exp(acl)
                yd = []
                for h in (h0, h0 + 1):
                    decay = jnp.exp(jnp.where(causal, acum[:, h:h + 1] - act[h:h + 1, :], -1e30))
                    yd.append(_dot((cb * decay).astype(BF16), dtxb))
                y = jnp.where(first_head, yd[0], yd[1]) + y_off + dx_ref[:, lo:lo + LANES] * xs
                yg = y * _silu(zbuf[r0:r0 + CHUNK, lo:lo + LANES])
                ygbuf[:, lo:lo + LANES] = yg
                ss = ss + jnp.sum(yg * yg, -1, keepdims=True)
                ht[:, lo:lo + LANES] = hprev * jnp.exp(a_last) + _dot(bmt, (dtx * jnp.exp(a_last - acl)).astype(BF16))
        inv = lax.rsqrt(ss * (1.0 / SSD_W) + LN_EPS)
        ybf[r0:r0 + CHUNK, :] = (ygbuf[...] * inv * ng_ref[...]).astype(BF16)

    mix = _dot(abf[...], wout_ref[0:CONF_W, :]) + _dot(ybf[...], wout_ref[CONF_W:, :])
    o_ref[0] = _layer_norm(ALPHA * x_ref[0] + mix, g_ref[...], b_ref[...])

    @pl.when(t == pl.num_programs(1) - 1)
    def _():
        for r in range(CONF_KERNEL - 1):
            src_row = tt + _permuted_row(CHUNK - (CONF_KERNEL - 1) + r)
            cst_ref[0, r:r + 1, :] = abuf[src_row:src_row + 1, :]
        for r in range(SSD_CONV - 1):
            src_row = tt + _permuted_row(CHUNK - (SSD_CONV - 1) + r)
            sst_ref[0, r:r + 1, :] = xbuf[src_row:src_row + 1, :]
        hst_ref[0] = ht[...].T.reshape(SSD_HEADS, SSD_HEAD_DIM, SSD_STATE)

    abuf[0:CHUNK, :] = abuf[tt:tt + CHUNK, :]
    xbuf[0:CHUNK, :] = xbuf[tt:tt + CHUNK, :]


def _prompt_mixer(x, win, wout, cw, cb, cg, cbeta, sw, sb, dtb, alog, dx, ng, g, b, tt):
    bsz, t, d = x.shape
    tile = pl.BlockSpec((1, tt, d), lambda bi, i: (bi, i, 0))
    per_b = lambda *s: pl.BlockSpec((1,) + s, lambda bi, i: (bi,) + (0,) * len(s))
    consts = (win, wout, cw, cb, cg, cbeta, sw, sb, dtb, alog, dx, ng, g, b)
    return pl.pallas_call(
        functools.partial(_prompt_mixer_kernel, tt),
        grid=(bsz, t // tt),
        in_specs=[tile] + [_resident(a.shape) for a in consts],
        out_specs=[tile, per_b(CONF_KERNEL - 1, CONF_W), per_b(SSD_CONV - 1, XBC_W),
                   per_b(SSD_HEADS, SSD_HEAD_DIM, SSD_STATE)],
        out_shape=[jax.ShapeDtypeStruct((bsz, t, d), F32),
                   jax.ShapeDtypeStruct((bsz, CONF_KERNEL - 1, CONF_W), F32),
                   jax.ShapeDtypeStruct((bsz, SSD_CONV - 1, XBC_W), F32),
                   jax.ShapeDtypeStruct((bsz, SSD_HEADS, SSD_HEAD_DIM, SSD_STATE), F32)],
        scratch_shapes=[pltpu.VMEM((CHUNK + tt, CONF_W), F32),
                        pltpu.VMEM((CHUNK + tt, XBC_W), F32),
                        pltpu.VMEM((tt, SSD_W), F32),
                        pltpu.VMEM((tt, LANES), F32),
                        pltpu.VMEM((8 * (CONF_KERNEL - 1) + CHUNK, CONF_W), F32),
                        pltpu.VMEM((8 * (SSD_CONV - 1) + CHUNK, XBC_W), F32),
                        pltpu.VMEM((CHUNK, CONF_W), F32),
                        pltpu.VMEM((CHUNK, XBC_W), F32),
                        pltpu.VMEM((CHUNK, SSD_W), F32),
                        pltpu.VMEM((tt, CONF_W), BF16),
                        pltpu.VMEM((tt, SSD_W), BF16),
                        pltpu.VMEM((SSD_STATE, SSD_W), F32)],
        compiler_params=_cparams(2),
        name="prompt_mixer",
    )(x, *consts)


def _proj_ln_kernel(n_in, *refs):
    in_refs, w_refs = refs[:n_in], refs[n_in:2 * n_in]
    x_ref, g_ref, b_ref, o_ref = refs[2 * n_in:]
    acc = ALPHA * x_ref[...]
    for a_ref, w_ref in zip(in_refs, w_refs):
        acc = acc + _dot(a_ref[...].astype(BF16), w_ref[...])
    o_ref[...] = _layer_norm(acc, g_ref[...], b_ref[...])


def _proj_ln(ins, ws, resid, g, b, tm):
    m, d = resid.shape
    n_in = len(ins)
    rows = lambda a: pl.BlockSpec((tm, a.shape[1]), lambda i: (i, 0))
    return pl.pallas_call(
        functools.partial(_proj_ln_kernel, n_in),
        grid=(m // tm,),
        in_specs=[rows(a) for a in ins] + [_resident(w.shape) for w in ws]
        + [rows(resid), _resident(g.shape), _resident(b.shape)],
        out_specs=rows(resid),
        out_shape=jax.ShapeDtypeStruct((m, d), F32),
        compiler_params=_cparams(1),
        name="proj_ln",
    )(*ins, *ws, resid, g, b)


def _prompt_xattn_kernel(x_ref, k_ref, v_ref, wq_ref, wo_ref, g_ref, b_ref, o_ref):
    x = x_ref[0]
    q = _dot(x.astype(BF16), wq_ref[...])
    acc = ALPHA * x
    for hd in range(XA_HEADS):
        sl = slice(hd * XA_HEAD_DIM, (hd + 1) * XA_HEAD_DIM)
        qh = q[:, sl].astype(BF16)
        kh = k_ref[0, :, sl].astype(BF16)
        s = lax.dot_general(qh, kh, (((1,), (1,)), ((), ())), preferred_element_type=F32)
        s = s * (XA_HEAD_DIM ** -0.5)
        e = jnp.exp(s - jnp.max(s, -1, keepdims=True))
        p = e / jnp.sum(e, -1, keepdims=True)
        oh = _dot(p.astype(BF16), v_ref[0, :, sl].astype(BF16))
        acc = acc + _dot(oh.astype(BF16), wo_ref[sl, :])
    o_ref[0] = _layer_norm(acc, g_ref[...], b_ref[...])


def _prompt_xattn(x, mk, mv, layer, wq, wo, g, b, tt):
    bsz, t, d = x.shape
    tile = pl.BlockSpec((1, tt, d), lambda bi, i: (bi, i, 0))
    mem = pl.BlockSpec((None, 1, MEM_LEN, d), lambda bi, i: (layer, bi, 0, 0))
    return pl.pallas_call(
        _prompt_xattn_kernel,
        grid=(bsz, t // tt),
        in_specs=[tile, mem, mem, _resident(wq.shape), _resident(wo.shape), _resident(g.shape),
                  _resident(b.shape)],
        out_specs=tile,
        out_shape=jax.ShapeDtypeStruct((bsz, t, d), F32),
        compiler_params=_cparams(2),
        name="prompt_xattn",
    )(x, mk, mv, wq, wo, g, b)


def _prompt_ffn_kernel(tt, x_ref, wup_ref, wdn_ref, cw_ref, cb_ref, g_ref, b_ref, o_ref, st_ref, hst):
    t = pl.program_id(1)
    nch = tt // CHUNK

    @pl.when(t == 0)
    def _():
        hst[...] = jnp.zeros(hst.shape, F32)

    xb = x_ref[0].astype(BF16)
    sub = lax.broadcasted_iota(jnp.int32, (8, FFN_COLS), 0)
    acc = jnp.zeros((tt, D_MODEL), F32)
    for j in range(D_FF // FFN_COLS):
        conv = []
        for half in range(2):
            c0 = half * D_FF + j * FFN_COLS
            cs = slice(c0, c0 + FFN_COLS)
            u = _dot(xb, wup_ref[:, cs])
            w0, w1, w2 = cw_ref[0:1, cs], cw_ref[1:2, cs], cw_ref[2:3, cs]
            outs = []
            for c in range(nch):
                cur = u[c * CHUNK:(c + 1) * CHUNK, :]
                prev = hst[:, cs] if c == 0 else u[c * CHUNK - 16:c * CHUNK, :]
                e2 = _wrapped_slab(cur[CHUNK - 16:CHUNK - 8, :], prev[0:8, :], 1, sub)
                e1 = _wrapped_slab(cur[CHUNK - 8:CHUNK, :], prev[8:16, :], 1, sub)
                back1 = jnp.concatenate([e1, cur[0:CHUNK - 8, :]], axis=0)
                back2 = jnp.concatenate([e2, e1, cur[0:CHUNK - 16, :]], axis=0)
                outs.append(cb_ref[:, cs] + w2 * cur + w1 * back1 + w0 * back2)
            hst[:, cs] = u[tt - 16:tt, :]
            conv.append(jnp.concatenate(outs, axis=0))
        f = _silu(conv[1]) * conv[0]
        acc = acc + _dot(f.astype(BF16), wdn_ref[j * FFN_COLS:(j + 1) * FFN_COLS, :])
    o_ref[0] = _layer_norm(ALPHA * x_ref[0] + acc, g_ref[...], b_ref[...])
    st_ref[0, 0:1, :] = hst[7:8, :]
    st_ref[0, 1:2, :] = hst[15:16, :]


def _prompt_ffn(x, wup, wdn, cw, cb, g, b, tt):
    bsz, t, d = x.shape
    tile = pl.BlockSpec((1, tt, d), lambda bi, i: (bi, i, 0))
    return pl.pallas_call(
        functools.partial(_prompt_ffn_kernel, tt),
        grid=(bsz, t // tt),
        in_specs=[tile, _resident(wup.shape), _resident(wdn.shape), _resident(cw.shape),
                  _resident(cb.shape), _resident(g.shape), _resident(b.shape)],
        out_specs=[tile, pl.BlockSpec((1, FFN_CONV - 1, 2 * D_FF), lambda bi, i: (bi, 0, 0))],
        out_shape=[jax.ShapeDtypeStruct((bsz, t, d), F32),
                   jax.ShapeDtypeStruct((bsz, FFN_CONV - 1, 2 * D_FF), F32)],
        scratch_shapes=[pltpu.VMEM((16, 2 * D_FF), F32)],
        compiler_params=_cparams(2),
        name="prompt_ffn",
    )(x, wup, wdn, cw, cb, g, b)


def _mem_proj_kernel(x_ref, w_ref, o_ref):
    o_ref[...] = _dot(x_ref[...].astype(BF16), w_ref[...])


def _mem_proj(mem, w, tm):
    m, d = mem.shape
    nl, _, n = w.shape
    return pl.pallas_call(
        _mem_proj_kernel,
        grid=(nl, m // tm),
        in_specs=[pl.BlockSpec((tm, d), lambda l, i: (i, 0)),
                  pl.BlockSpec((None, d, n), lambda l, i: (l, 0, 0))],
        out_specs=pl.BlockSpec((None, tm, n), lambda l, i: (l, i, 0)),
        out_shape=jax.ShapeDtypeStruct((nl, m, n), F32),
        compiler_params=_cparams(2),
        name="mem_proj",
    )(mem, w)


def _mm_k_kernel(x_ref, w_ref, o_ref):
    @pl.when(pl.program_id(0) == 0)
    def _():
        o_ref[...] = jnp.zeros(o_ref.shape, F32)

    o_ref[...] += _dot(x_ref[...].astype(BF16), w_ref[...])


def _mm_k(x, w, tk=256):
    m, k = x.shape
    n = w.shape[1]
    return pl.pallas_call(
        _mm_k_kernel,
        grid=(k // tk,),
        in_specs=[pl.BlockSpec((m, tk), lambda i: (0, i)), pl.BlockSpec((tk, n), lambda i: (i, 0))],
        out_specs=pl.BlockSpec((m, n), lambda i: (0, 0)),
        out_shape=jax.ShapeDtypeStruct((m, n), F32),
        compiler_params=_cparams(1),
        name="mm_k",
    )(x, w)


def _sample_conf_kernel(h_ref, st_ref, cw_ref, cb_ref, g_ref, b_ref, a_ref, nst_ref):
    glu = h_ref[...]
    a = glu[:, :CONF_W] * _sigmoid(glu[:, CONF_W:])
    nk = CONF_KERNEL - 1
    acc = cb_ref[...] + cw_ref[nk:nk + 1, :] * a
    for k in range(nk):
        acc = acc + cw_ref[k:k + 1, :] * st_ref[:, k * CONF_W:(k + 1) * CONF_W]
    a_ref[...] = _silu(_layer_norm(acc, g_ref[...], b_ref[...])).astype(BF16)
    nst_ref[:, 0:(nk - 1) * CONF_W] = st_ref[:, CONF_W:nk * CONF_W]
    nst_ref[:, (nk - 1) * CONF_W:nk * CONF_W] = a


def _sample_conf(h, st, layer, cw, cb, g, b, bb=32):
    n = h.shape[0]
    sw = (CONF_KERNEL - 1) * CONF_W
    return pl.pallas_call(
        _sample_conf_kernel,
        grid=(n // bb,),
        in_specs=[pl.BlockSpec((bb, 2 * CONF_W), lambda i: (i, 0)),
                  pl.BlockSpec((None, bb, sw), lambda i: (layer, i, 0)),
                  _resident(cw.shape), _resident(cb.shape), _resident(g.shape), _resident(b.shape)],
        out_specs=[pl.BlockSpec((bb, CONF_W), lambda i: (i, 0)), pl.BlockSpec((bb, sw), lambda i: (i, 0))],
        out_shape=[jax.ShapeDtypeStruct((n, CONF_W), BF16), jax.ShapeDtypeStruct((n, sw), F32)],
        compiler_params=_cparams(1),
        name="sample_conf",
    )(h, st, cw, cb, g, b)


def _sample_ssd_prep_kernel(xbc_ref, dt_ref, st_ref, sw_ref, sb_ref, dtb_ref, alog_ref, rexp_ref,
                            nst_ref, xs_ref, bc_ref, xdt_t_ref, decay_ref):
    xr = xbc_ref[...]
    nk = SSD_CONV - 1
    acc = sb_ref[...] + sw_ref[nk:nk + 1, :] * xr
    for k in range(nk):
        acc = acc + sw_ref[k:k + 1, :] * st_ref[:, k * XBC_W:(k + 1) * XBC_W]
    xbc = _silu(acc)
    nst_ref[:, 0:(nk - 1) * XBC_W] = st_ref[:, XBC_W:nk * XBC_W]
    nst_ref[:, (nk - 1) * XBC_W:nk * XBC_W] = xr
    xs = xbc[:, :SSD_W]
    xs_ref[...] = xs
    bc_ref[...] = xbc[:, SSD_W:]
    dt = _softplus(dt_ref[...] + dtb_ref[...])
    rexp = rexp_ref[...]
    a_neg = -jnp.exp(alog_ref[...])
    xdt_t_ref[...] = (xs * _dot_f32(dt, rexp)).T
    decay_ref[...] = jnp.exp(dt * a_neg)


def _sample_ssd_prep(h, st, layer, sw, sb, dtb, alog, rexp):
    n = h.shape[0]
    stw = (SSD_CONV - 1) * XBC_W
    full = lambda r, c: pl.BlockSpec((r, c), lambda i: (0, 0))
    return pl.pallas_call(
        _sample_ssd_prep_kernel,
        grid=(1,),
        in_specs=[pl.BlockSpec((n, XBC_W), lambda i: (0, C_XBC // XBC_W)),
                  pl.BlockSpec((n, LANES), lambda i: (0, C_DT // LANES)),
                  pl.BlockSpec((None, n, stw), lambda i: (layer, 0, 0)),
                  full(*sw.shape), full(*sb.shape), full(*dtb.shape), full(*alog.shape), full(*rexp.shape)],
        out_specs=[full(n, stw), full(n, SSD_W), full(n, XBC_W - SSD_W), full(SSD_W, n), full(n, LANES)],
        out_shape=[jax.ShapeDtypeStruct((n, stw), F32), jax.ShapeDtypeStruct((n, SSD_W), F32),
                   jax.ShapeDtypeStruct((n, XBC_W - SSD_W), F32), jax.ShapeDtypeStruct((SSD_W, n), F32),
                   jax.ShapeDtypeStruct((n, LANES), F32)],
        compiler_params=_cparams(1),
        name="sample_ssd_prep",
    )(h, h, st, sw, sb, dtb, alog, rexp)


def _sample_ssd_state_kernel(n, decay_ref, st_ref, xdt_t_ref, b_ref, c_ref, nst_ref, y_t_ref):
    h = pl.program_id(0)
    lane = lax.broadcasted_iota(jnp.int32, (SSD_HEAD_DIM, n), 1)
    y_t = jnp.zeros((SSD_HEAD_DIM, n), F32)
    for b in range(n):
        xcol = jnp.sum(jnp.where(lane == b, xdt_t_ref[...], 0.0), axis=-1, keepdims=True)
        hn = st_ref[b] * decay_ref[b, h] + xcol * b_ref[b:b + 1, :]
        nst_ref[b] = hn
        y_t = jnp.where(lane == b, jnp.sum(hn * c_ref[b:b + 1, :], axis=-1, keepdims=True), y_t)
    y_t_ref[...] = y_t


def _sample_ssd_state(st, layer, decay, xdt_t, bc):
    n = st.shape[1]
    hpg = SSD_HEADS // SSD_GROUPS
    col = pl.BlockSpec((SSD_HEAD_DIM, n), lambda h: (h, 0))
    return pl.pallas_call(
        functools.partial(_sample_ssd_state_kernel, n),
        grid=(SSD_HEADS,),
        in_specs=[pl.BlockSpec(memory_space=pltpu.SMEM),
                  pl.BlockSpec((None, n, None, SSD_HEAD_DIM, SSD_STATE), lambda h: (layer, 0, h, 0, 0)),
                  col,
                  pl.BlockSpec((n, SSD_STATE), lambda h: (0, h // hpg)),
                  pl.BlockSpec((n, SSD_STATE), lambda h: (0, SSD_GROUPS + h // hpg))],
        out_specs=[pl.BlockSpec((n, None, SSD_HEAD_DIM, SSD_STATE), lambda h: (0, h, 0, 0)), col],
        out_shape=[jax.ShapeDtypeStruct((n, SSD_HEADS, SSD_HEAD_DIM, SSD_STATE), F32),
                   jax.ShapeDtypeStruct((SSD_W, n), F32)],
        compiler_params=_cparams(1),
        name="sample_ssd_state",
    )(decay, st, xdt_t, bc, bc)


def _sample_mix_out_kernel(y_t_ref, xs_ref, z_ref, a_ref, x_ref, wo_ref, dx_ref, ng_ref, g_ref, b_ref, o_ref):
    y = y_t_ref[...].T + dx_ref[...] * xs_ref[...]
    y = y * _silu(z_ref[...])
    ms = jnp.mean(y * y, -1, keepdims=True)
    y = y * lax.rsqrt(ms + LN_EPS) * ng_ref[...]
    acc = ALPHA * x_ref[...] + _dot(a_ref[...], wo_ref[0:CONF_W, :]) + _dot(y.astype(BF16), wo_ref[CONF_W:, :])
    o_ref[...] = _layer_norm(acc, g_ref[...], b_ref[...])


def _sample_mix_out(y_t, xs, h, a, x, wo, dx, ng, g, b):
    n = x.shape[0]
    full = lambda arr: pl.BlockSpec(arr.shape, lambda i: (0,) * arr.ndim)
    return pl.pallas_call(
        _sample_mix_out_kernel,
        grid=(1,),
        in_specs=[full(y_t), full(xs), pl.BlockSpec((n, SSD_W), lambda i: (0, C_Z // SSD_W)), full(a),
                  full(x), full(wo), full(dx), full(ng), full(g), full(b)],
        out_specs=full(x),
        out_shape=jax.ShapeDtypeStruct(x.shape, F32),
        compiler_params=_cparams(1),
        name="sample_mix_out",
    )(y_t, xs, h, a, x, wo, dx, ng, g, b)


def _flat_cache(c):
    nl, n = c.shape[0], c.shape[1]
    c = c.reshape(nl, n, MEM_LEN, XA_HEADS, XA_HEAD_DIM // LANES, LANES)
    return c.transpose(0, 1, 2, 4, 3, 5).reshape(nl, n, MEM_LEN * 2 * XA_HEADS, LANES)


def _sample_attn_kernel(bb, q_ref, k_ref, v_ref, o_ref):
    nhalf = XA_HEAD_DIM // LANES
    rows_per_m = nhalf * XA_HEADS
    for j in range(bb):
        outs = []
        for hd in range(XA_HEADS):
            prod = None
            for half in range(nhalf):
                kk = k_ref[j, pl.ds(half * XA_HEADS + hd, MEM_LEN, stride=rows_per_m), :]
                lo = hd * XA_HEAD_DIM + half * LANES
                part = kk * q_ref[j, :, lo:lo + LANES]
                prod = part if prod is None else prod + part
            s = jnp.sum(prod, axis=-1, keepdims=True) * (XA_HEAD_DIM ** -0.5)
            e = jnp.exp(s - jnp.max(s, axis=0, keepdims=True))
            p = e / jnp.sum(e, axis=0, keepdims=True)
            for half in range(nhalf):
                vv = v_ref[j, pl.ds(half * XA_HEADS + hd, MEM_LEN, stride=rows_per_m), :]
                outs.append(jnp.sum(p * vv, axis=0, keepdims=True))
        o_ref[j] = jnp.concatenate(outs, axis=-1)


def _sample_attn(q, mk, mv, layer, bb=4):
    n, d = q.shape
    row = pl.BlockSpec((bb, 1, d), lambda i: (i, 0, 0))
    mem = pl.BlockSpec((None, bb, mk.shape[2], LANES), lambda i: (layer, i, 0, 0))
    o = pl.pallas_call(
        functools.partial(_sample_attn_kernel, bb),
        grid=(n // bb,),
        in_specs=[row, mem, mem],
        out_specs=row,
        out_shape=jax.ShapeDtypeStruct((n, 1, d), F32),
        compiler_params=_cparams(1),
        name="sample_attn",
    )(q.reshape(n, 1, d), mk, mv)
    return o.reshape(n, d)


def _sample_ffn_kernel(u_ref, st_ref, cw_ref, cb_ref, x_ref, wdn_ref, g_ref, b_ref, o_ref, nst_ref):
    w2 = 2 * D_FF
    u = u_ref[...]
    cv = cb_ref[...] + cw_ref[0:1, :] * st_ref[:, 0:w2] + cw_ref[1:2, :] * st_ref[:, w2:2 * w2] \
        + cw_ref[2:3, :] * u
    nst_ref[:, 0:w2] = st_ref[:, w2:2 * w2]
    nst_ref[:, w2:2 * w2] = u
    f = _silu(cv[:, D_FF:]) * cv[:, :D_FF]
    acc = ALPHA * x_ref[...] + _dot(f.astype(BF16), wdn_ref[...])
    o_ref[...] = _layer_norm(acc, g_ref[...], b_ref[...])


def _sample_ffn(u, st, layer, cw, cb, x, wdn, g, b):
    n = x.shape[0]
    stw = (FFN_CONV - 1) * 2 * D_FF
    one = lambda arr: pl.BlockSpec(arr.shape, lambda i: (0,) * arr.ndim, pipeline_mode=pl.Buffered(1))
    return pl.pallas_call(
        _sample_ffn_kernel,
        grid=(1,),
        in_specs=[one(u), pl.BlockSpec((None, n, stw), lambda i: (layer, 0, 0), pipeline_mode=pl.Buffered(1)),
                  one(cw), one(cb), one(x), one(wdn), one(g), one(b)],
        out_specs=[pl.BlockSpec(x.shape, lambda i: (0, 0)), pl.BlockSpec((n, stw), lambda i: (0, 0))],
        out_shape=[jax.ShapeDtypeStruct(x.shape, F32), jax.ShapeDtypeStruct((n, stw), F32)],
        compiler_params=_cparams(1),
        name="sample_ffn",
    )(u, st, cw, cb, x, wdn, g, b)


def kernel(x_prompt, x_sample, cache_mem_k, cache_mem_v, state_conf_conv, state_ssd_conv, state_ssd,
           state_ffn_conv, mem_prompt, w_in, conf_conv_w, conf_conv_b, conf_ln_g, conf_ln_b, ssd_conv_w,
           ssd_conv_b, ssd_dt_bias, ssd_a_log, ssd_d, ssd_norm_g, w_out, ln_mix_g, ln_mix_b, xa_wq, xa_wk,
           xa_wv, xa_wo, ln_xa_g, ln_xa_b, ffn_w_up, ffn_conv_w, ffn_conv_b, ffn_w_down, ln_ffn_g, ln_ffn_b):
    bp, t, d = x_prompt.shape
    ns = x_sample.shape[0]
    nl = DEPTH
    tt_mix, tt = min(MIXER_TILE, t), min(ROW_TILE, t)

    row = lambda p, l: p[l][None, :]
    pad_heads = lambda p, l: jnp.pad(p[l], (0, LANES - SSD_HEADS))[None, :]
    rexp = jnp.pad(jnp.repeat(jnp.eye(SSD_HEADS, dtype=F32), SSD_HEAD_DIM, axis=1),
                   ((0, LANES - SSD_HEADS), (0, 0)))

    mem2d = mem_prompt.reshape(bp * MEM_LEN, d)
    mk_p = _mem_proj(mem2d, xa_wk.astype(BF16), 512).reshape(nl, bp, MEM_LEN, d)
    mv_p = _mem_proj(mem2d, xa_wv.astype(BF16), 512).reshape(nl, bp, MEM_LEN, d)
    mk_s, mv_s = _flat_cache(cache_mem_k), _flat_cache(cache_mem_v)
    conf_s = state_conf_conv.reshape(nl, ns, (CONF_KERNEL - 1) * CONF_W)
    ssdc_s = state_ssd_conv.reshape(nl, ns, (SSD_CONV - 1) * XBC_W)
    ffn_s = state_ffn_conv.reshape(nl, ns, (FFN_CONV - 1) * 2 * D_FF)

    xp = _permute_rows(x_prompt, inverse=False)
    xs_ = x_sample.reshape(ns, d)
    st_p = ([], [], [], [])
    st_s = ([], [], [], [])
    for l in range(nl):
        win = jnp.pad(w_in[l], ((0, 0), (0, D_IN_PAD - w_in.shape[2]))).astype(BF16)
        wout = w_out[l].astype(BF16)
        wq, wo = xa_wq[l].astype(BF16), xa_wo[l].astype(BF16)
        wup, wdn = ffn_w_up[l].astype(BF16), ffn_w_down[l].astype(BF16)
        cw, cb = conf_conv_w[l], row(conf_conv_b, l)
        cg, cbeta = row(conf_ln_g, l), row(conf_ln_b, l)
        sw, sb = ssd_conv_w[l], row(ssd_conv_b, l)
        dtb, alog = pad_heads(ssd_dt_bias, l), pad_heads(ssd_a_log, l)
        dx = jnp.repeat(ssd_d[l], SSD_HEAD_DIM)[None, :]
        ng = row(ssd_norm_g, l)
        g1, b1 = row(ln_mix_g, l), row(ln_mix_b, l)
        g2, b2 = row(ln_xa_g, l), row(ln_xa_b, l)
        g3, b3 = row(ln_ffn_g, l), row(ln_ffn_b, l)
        fcw, fcb = ffn_conv_w[l], row(ffn_conv_b, l)

        x1, cst, sst, hst = _prompt_mixer(xp, win, wout, cw, cb, cg, cbeta, sw, sb, dtb, alog, dx, ng, g1, b1, tt_mix)
        x2 = _prompt_xattn(x1, mk_p, mv_p, l, wq, wo, g2, b2, tt)
        xp, fst = _prompt_ffn(x2, wup, wdn, fcw, fcb, g3, b3, tt)
        for lst, s in zip(st_p, (cst, sst, hst, fst)):
            lst.append(s)

        h = _mm_k(xs_, win)
        a_s, ncs = _sample_conf(h, conf_s, l, cw, cb, cg, cbeta)
        nss, xs1, bc, xdt_t, decay = _sample_ssd_prep(h, ssdc_s, l, sw, sb, dtb, alog, rexp)
        nst, y_t = _sample_ssd_state(state_ssd, l, decay[:, :SSD_HEADS], xdt_t, bc)
        x1s = _sample_mix_out(y_t, xs1, h, a_s, xs_, wout, dx, ng, g1, b1)
        qs = _mm_k(x1s, wq)
        os_ = _sample_attn(qs, mk_s, mv_s, l)
        x2s = _proj_ln([os_], [wo], x1s, g2, b2, ns)
        us = _mm_k(x2s, wup)
        xs_, nfs = _sample_ffn(us, ffn_s, l, fcw, fcb, x2s, wdn, g3, b3)
        for lst, s in zip(st_s, (ncs, nss, nst, nfs)):
            lst.append(s)

    st_p = [jnp.stack(s) for s in st_p]
    st_s = [jnp.stack(s) for s in st_s]
    hd = (XA_HEADS, XA_HEAD_DIM)
    return (_permute_rows(xp, inverse=True), xs_.reshape(ns, 1, d), st_p[0], st_p[1], st_p[2], st_p[3],
            mk_p.reshape(nl, bp, MEM_LEN, *hd), mv_p.reshape(nl, bp, MEM_LEN, *hd),
            st_s[0].reshape(nl, ns, CONF_KERNEL - 1, CONF_W), st_s[1].reshape(nl, ns, SSD_CONV - 1, XBC_W),
            st_s[2], st_s[3].reshape(nl, ns, FFN_CONV - 1, 2 * D_FF))
```

```python
import functools

import jax
import jax.numpy as jnp
from jax import lax
from jax.experimental import pallas as pl
from jax.experimental.pallas import tpu as pltpu

F32 = jnp.float32
BF16 = jnp.bfloat16
HIGHEST = lax.Precision.HIGHEST

D_MODEL = 1024
DEPTH = 4
CONF_W = 1024
CONF_KERNEL = 31
SSD_HEAD_DIM = 64
SSD_HEADS = 16
SSD_W = 1024
SSD_GROUPS = 2
SSD_STATE = 128
SSD_CONV = 4
SSD_CHUNK = 128
XBC_W = SSD_W + 2 * SSD_GROUPS * SSD_STATE
MEM_LEN = 256
XA_HEADS = 4
XA_HEAD_DIM = 256
D_FF = 2816
FFN_CONV = 3
ALPHA = (2.0 * DEPTH) ** 0.25
LN_EPS = 1e-5

LANES = 128
C_GLU, C_Z, C_XBC, C_DT = 0, 2 * CONF_W, 2 * CONF_W + SSD_W, 2 * CONF_W + SSD_W + XBC_W
D_IN_PAD = C_DT + LANES
MIXER_TILE = 256
ROW_TILE = 512
FFN_COLS = 256
VMEM_LIMIT = 56 * 1024 * 1024


def _cparams(n_grid_dims):
    return pltpu.CompilerParams(dimension_semantics=("arbitrary",) * n_grid_dims,
                                vmem_limit_bytes=VMEM_LIMIT)


def _resident(shape):
    nd = len(shape)
    return pl.BlockSpec(shape, lambda *_: (0,) * nd, pipeline_mode=pl.Buffered(1))


def _layer_resident(w, layer):
    return pl.BlockSpec((None,) + w.shape[1:], lambda *_: (layer, 0, 0), pipeline_mode=pl.Buffered(1))


def _sigmoid(x):
    return jax.nn.sigmoid(x)


def _silu(x):
    return x * _sigmoid(x)


def _softplus(x):
    return jnp.maximum(x, 0.0) + jnp.log(1.0 + jnp.exp(-jnp.abs(x)))


def _layer_norm(x, g, b):
    mu = jnp.mean(x, -1, keepdims=True)
    xc = x - mu
    var = jnp.mean(xc * xc, -1, keepdims=True)
    return xc * lax.rsqrt(var + LN_EPS) * g + b


def _dot(a, b):
    return jnp.dot(a, b, preferred_element_type=F32)


def _dot_f32(a, b):
    return jnp.dot(a, b, preferred_element_type=F32, precision=HIGHEST)


CHUNK = SSD_CHUNK
SLABS = CHUNK // 8


def _permuted_row(token):
    return 8 * (token % SLABS) + token // SLABS


def _wrapped_slab(cur, prev, shift, sub):
    return jnp.where(sub >= shift, pltpu.roll(cur, shift, 0), pltpu.roll(prev, shift, 0))


def _fill_history(src, base, eseq, nk, width):
    sub = lax.broadcasted_iota(jnp.int32, (8, width), 0)
    for i in range(SLABS):
        for shift in (1, 2):
            p = i - SLABS * shift + (nk - 1)
            if 0 <= p < nk - 1:
                cur = src[base + 8 * i:base + 8 * (i + 1), :]
                prev = src[base - CHUNK + 8 * i:base - CHUNK + 8 * (i + 1), :]
                eseq[8 * p:8 * (p + 1), :] = _wrapped_slab(cur, prev, shift, sub)
    eseq[8 * (nk - 1):8 * (nk - 1) + CHUNK, :] = src[base:base + CHUNK, :]


def _conv_from_history(eseq, w_ref, nk, lo, rows):
    r0, nr = rows
    acc = w_ref[0:1, lo:lo + LANES] * eseq[r0:r0 + nr, lo:lo + LANES]
    for k in range(1, nk):
        acc = acc + w_ref[k:k + 1, lo:lo + LANES] * eseq[8 * k + r0:8 * k + r0 + nr, lo:lo + LANES]
    return acc


def _permute_rows_kernel(tt, inverse, *refs):
    in_refs, o_ref = refs[:-1], refs[-1]
    for c in range(tt // CHUNK):
        for lt, x_ref in enumerate(in_refs):
            ls = slice(lt * LANES, (lt + 1) * LANES)
            if not inverse:
                for i in range(SLABS):
                    o_ref[0, c * CHUNK + 8 * i:c * CHUNK + 8 * (i + 1), ls] = \
                        x_ref[0, pl.ds(c * CHUNK + i, 8, stride=SLABS), :]
            else:
                for s in range(8):
                    for i0 in range(0, SLABS, 8):
                        tok = c * CHUNK + s * SLABS + i0
                        o_ref[0, tok:tok + 8, ls] = x_ref[0, pl.ds(c * CHUNK + 8 * i0 + s, 8, stride=8), :]


def _permute_rows(x, inverse, tt=512):
    bsz, t, w = x.shape
    tt = min(tt, t)
    lane_tile = lambda lt: pl.BlockSpec((1, tt, LANES), lambda bi, i: (bi, i, lt))
    return pl.pallas_call(
        functools.partial(_permute_rows_kernel, tt, inverse),
        grid=(bsz, t // tt),
        in_specs=[lane_tile(lt) for lt in range(w // LANES)],
        out_specs=pl.BlockSpec((1, tt, w), lambda bi, i: (bi, i, 0)),
        out_shape=jax.ShapeDtypeStruct(x.shape, x.dtype),
        compiler_params=_cparams(2),
        name="permute_rows",
    )(*([x] * (w // LANES)))


def _permuted_time(n):
    r = lax.broadcasted_iota(jnp.int32, (CHUNK, CHUNK), n)
    return (r & 7) * SLABS + (r >> 3)


def _prompt_mixer_kernel(tt, x_ref, win_ref, wout_ref, cw_ref, cb_ref, cg_ref, cbeta_ref, sw_ref, sb_ref,
                         dtb_ref, alog_ref, dx_ref, ng_ref, g_ref, b_ref,
                         o_ref, cst_ref, sst_ref, hst_ref,
                         abuf, xbuf, zbuf, dtbuf, eseq, eseq2, cvbuf, xact, ygbuf, abf, ybf, ht):
    t = pl.program_id(1)
    nch = tt // CHUNK

    @pl.when(t == 0)
    def _():
        abuf[0:CHUNK, :] = jnp.zeros((CHUNK, CONF_W), F32)
        xbuf[0:CHUNK, :] = jnp.zeros((CHUNK, XBC_W), F32)
        ht[...] = jnp.zeros(ht.shape, F32)

    xb = x_ref[0].astype(BF16)
    glu = _dot(xb, win_ref[:, C_GLU:C_GLU + 2 * CONF_W])
    abuf[CHUNK:CHUNK + tt, :] = glu[:, :CONF_W] * _sigmoid(glu[:, CONF_W:])
    zbuf[...] = _dot(xb, win_ref[:, C_Z:C_Z + SSD_W])
    xbuf[CHUNK:CHUNK + tt, :] = _dot(xb, win_ref[:, C_XBC:C_XBC + XBC_W])
    dtbuf[...] = _softplus(_dot(xb, win_ref[:, C_DT:C_DT + LANES]) + dtb_ref[...])

    tl, ts = _permuted_time(0), _permuted_time(1)
    causal = tl >= ts
    causal_f = causal.astype(F32)
    a_neg = -jnp.exp(alog_ref[...])
    first_head = lax.broadcasted_iota(jnp.int32, (CHUNK, LANES), 1) < SSD_HEAD_DIM
    rb = 32

    for c in range(nch):
        base = CHUNK * (c + 1)
        r0 = CHUNK * c
        _fill_history(abuf, base, eseq, CONF_KERNEL, CONF_W)
        for lt in range(CONF_W // LANES):
            for r in range(CHUNK // rb):
                cvbuf[r * rb:(r + 1) * rb, lt * LANES:(lt + 1) * LANES] = _conv_from_history(
                    eseq, cw_ref, CONF_KERNEL, lt * LANES, (r * rb, rb))
        y = _layer_norm(cvbuf[...] + cb_ref[...], cg_ref[...], cbeta_ref[...])
        abf[r0:r0 + CHUNK, :] = _silu(y).astype(BF16)

        _fill_history(xbuf, base, eseq2, SSD_CONV, XBC_W)
        for lt in range(XBC_W // LANES):
            lo = lt * LANES
            acc = _conv_from_history(eseq2, sw_ref, SSD_CONV, lo, (0, CHUNK))
            xact[:, lo:lo + LANES] = _silu(acc + sb_ref[:, lo:lo + LANES])

        dt = dtbuf[r0:r0 + CHUNK, :]
        acum = _dot_f32(causal_f, dt * a_neg)
        act = acum.T
        ss = jnp.zeros((CHUNK, 1), F32)
        for g in range(SSD_GROUPS):
            bm = xact[:, SSD_W + g * SSD_STATE:SSD_W + (g + 1) * SSD_STATE]
            cmb = xact[:, SSD_W + (SSD_GROUPS + g) * SSD_STATE:SSD_W + (SSD_GROUPS + g + 1) * SSD_STATE].astype(BF16)
            cb = lax.dot_general(cmb, bm.astype(BF16), (((1,), (1,)), ((), ())), preferred_element_type=F32)
            bmt = bm.T.astype(BF16)
            for jj in range(SSD_W // SSD_GROUPS // LANES):
                lo = (g * (SSD_W // SSD_GROUPS // LANES) + jj) * LANES
                h0 = lo // SSD_HEAD_DIM
                dtl = jnp.where(first_head, dt[:, h0:h0 + 1], dt[:, h0 + 1:h0 + 2])
                acl = jnp.where(first_head, acum[:, h0:h0 + 1], acum[:, h0 + 1:h0 + 2])
                xs = xact[:, lo:lo + LANES]
                dtx = xs * dtl
                dtxb = dtx.astype(BF16)
                a_last = acl[CHUNK - 1:CHUNK, :]
                hprev = ht[:, lo:lo + LANES]
                y_off = _dot(cmb, hprev.astype(BF16)) * jnp.exp(acl)
                yd = []
                for h in (h0, h0 + 1):
                    decay = jnp.exp(jnp.where(causal, acum[:, h:h + 1] - act[h:h + 1, :], -1e30))
                    yd.append(_dot((cb * decay).astype(BF16), dtxb))
                y = jnp.where(first_head, yd[0], yd[1]) + y_off + dx_ref[:, lo:lo + LANES] * xs
                yg = y * _silu(zbuf[r0:r0 + CHUNK, lo:lo + LANES])
                ygbuf[:, lo:lo + LANES] = yg
                ss = ss + jnp.sum(yg * yg, -1, keepdims=True)
                ht[:, lo:lo + LANES] = hprev * jnp.exp(a_last) + _dot(bmt, (dtx * jnp.exp(a_last - acl)).astype(BF16))
        inv = lax.rsqrt(ss * (1.0 / SSD_W) + LN_EPS)
        ybf[r0:r0 + CHUNK, :] = (ygbuf[...] * inv * ng_ref[...]).astype(BF16)

    mix = _dot(abf[...], wout_ref[0:CONF_W, :]) + _dot(ybf[...], wout_ref[CONF_W:, :])
    o_ref[0] = _layer_norm(ALPHA * x_ref[0] + mix, g_ref[...], b_ref[...])

    @pl.when(t == pl.num_programs(1) - 1)
    def _():
        for r in range(CONF_KERNEL - 1):
            src_row = tt + _permuted_row(CHUNK - (CONF_KERNEL - 1) + r)
            cst_ref[0, r:r + 1, :] = abuf[src_row:src_row + 1, :]
        for r in range(SSD_CONV - 1):
            src_row = tt + _permuted_row(CHUNK - (SSD_CONV - 1) + r)
            sst_ref[0, r:r + 1, :] = xbuf[src_row:src_row + 1, :]
        hst_ref[0] = ht[...].T.reshape(SSD_HEADS, SSD_HEAD_DIM, SSD_STATE)

    abuf[0:CHUNK, :] = abuf[tt:tt + CHUNK, :]
    xbuf[0:CHUNK, :] = xbuf[tt:tt + CHUNK, :]


def _prompt_mixer(x, layer, win, wout, cw, cb, cg, cbeta, sw, sb, dtb, alog, dx, ng, g, b, tt):
    bsz, t, d = x.shape
    tile = pl.BlockSpec((1, tt, d), lambda bi, i: (bi, i, 0))
    per_b = lambda *s: pl.BlockSpec((1,) + s, lambda bi, i: (bi,) + (0,) * len(s))
    consts = (cw, cb, cg, cbeta, sw, sb, dtb, alog, dx, ng, g, b)
    return pl.pallas_call(
        functools.partial(_prompt_mixer_kernel, tt),
        grid=(bsz, t // tt),
        in_specs=[tile, _layer_resident(win, layer), _layer_resident(wout, layer)]
        + [_resident(a.shape) for a in consts],
        out_specs=[tile, per_b(CONF_KERNEL - 1, CONF_W), per_b(SSD_CONV - 1, XBC_W),
                   per_b(SSD_HEADS, SSD_HEAD_DIM, SSD_STATE)],
        out_shape=[jax.ShapeDtypeStruct((bsz, t, d), F32),
                   jax.ShapeDtypeStruct((bsz, CONF_KERNEL - 1, CONF_W), F32),
                   jax.ShapeDtypeStruct((bsz, SSD_CONV - 1, XBC_W), F32),
                   jax.ShapeDtypeStruct((bsz, SSD_HEADS, SSD_HEAD_DIM, SSD_STATE), F32)],
        scratch_shapes=[pltpu.VMEM((CHUNK + tt, CONF_W), F32),
                        pltpu.VMEM((CHUNK + tt, XBC_W), F32),
                        pltpu.VMEM((tt, SSD_W), F32),
                        pltpu.VMEM((tt, LANES), F32),
                        pltpu.VMEM((8 * (CONF_KERNEL - 1) + CHUNK, CONF_W), F32),
                        pltpu.VMEM((8 * (SSD_CONV - 1) + CHUNK, XBC_W), F32),
                        pltpu.VMEM((CHUNK, CONF_W), F32),
                        pltpu.VMEM((CHUNK, XBC_W), F32),
                        pltpu.VMEM((CHUNK, SSD_W), F32),
                        pltpu.VMEM((tt, CONF_W), BF16),
                        pltpu.VMEM((tt, SSD_W), BF16),
                        pltpu.VMEM((SSD_STATE, SSD_W), F32)],
        compiler_params=_cparams(2),
        name="prompt_mixer",
    )(x, win, wout, *consts)


def _proj_ln_kernel(n_in, *refs):
    in_refs, w_refs = refs[:n_in], refs[n_in:2 * n_in]
    x_ref, g_ref, b_ref, o_ref = refs[2 * n_in:]
    acc = ALPHA * x_ref[...]
    for a_ref, w_ref in zip(in_refs, w_refs):
        acc = acc + _dot(a_ref[...].astype(BF16), w_ref[...])
    o_ref[...] = _layer_norm(acc, g_ref[...], b_ref[...])


def _proj_ln(ins, ws, layer, resid, g, b, tm):
    m, d = resid.shape
    n_in = len(ins)
    rows = lambda a: pl.BlockSpec((tm, a.shape[1]), lambda i: (i, 0))
    return pl.pallas_call(
        functools.partial(_proj_ln_kernel, n_in),
        grid=(m // tm,),
        in_specs=[rows(a) for a in ins] + [_layer_resident(w, layer) for w in ws]
        + [rows(resid), _resident(g.shape), _resident(b.shape)],
        out_specs=rows(resid),
        out_shape=jax.ShapeDtypeStruct((m, d), F32),
        compiler_params=_cparams(1),
        name="proj_ln",
    )(*ins, *ws, resid, g, b)


def _prompt_xattn_kernel(x_ref, k_ref, v_ref, wq_ref, wo_ref, g_ref, b_ref, o_ref):
    x = x_ref[0]
    q = _dot(x.astype(BF16), wq_ref[...])
    acc = ALPHA * x
    for hd in range(XA_HEADS):
        sl = slice(hd * XA_HEAD_DIM, (hd + 1) * XA_HEAD_DIM)
        qh = q[:, sl].astype(BF16)
        kh = _load_flat_head(k_ref, 0, hd).astype(BF16)
        s = lax.dot_general(qh, kh, (((1,), (1,)), ((), ())), preferred_element_type=F32)
        s = s * (XA_HEAD_DIM ** -0.5)
        e = jnp.exp(s - jnp.max(s, -1, keepdims=True))
        p = e / jnp.sum(e, -1, keepdims=True)
        oh = _dot(p.astype(BF16), _load_flat_head(v_ref, 0, hd).astype(BF16))
        acc = acc + _dot(oh.astype(BF16), wo_ref[sl, :])
    o_ref[0] = _layer_norm(acc, g_ref[...], b_ref[...])


def _prompt_xattn(x, mk, mv, layer, wq, wo, g, b, tt):
    bsz, t, d = x.shape
    tile = pl.BlockSpec((1, tt, d), lambda bi, i: (bi, i, 0))
    mem = pl.BlockSpec((None, 1, MEM_ROWS, LANES), lambda bi, i: (layer, bi, 0, 0))
    return pl.pallas_call(
        _prompt_xattn_kernel,
        grid=(bsz, t // tt),
        in_specs=[tile, mem, mem, _layer_resident(wq, layer), _layer_resident(wo, layer), _resident(g.shape),
                  _resident(b.shape)],
        out_specs=tile,
        out_shape=jax.ShapeDtypeStruct((bsz, t, d), F32),
        compiler_params=_cparams(2),
        name="prompt_xattn",
    )(x, mk, mv, wq, wo, g, b)


def _prompt_ffn_kernel(tt, x_ref, wup_ref, wdn_ref, cw_ref, cb_ref, g_ref, b_ref, o_ref, st_ref, hst):
    t = pl.program_id(1)
    nch = tt // CHUNK

    @pl.when(t == 0)
    def _():
        hst[...] = jnp.zeros(hst.shape, F32)

    xb = x_ref[0].astype(BF16)
    sub = lax.broadcasted_iota(jnp.int32, (8, FFN_COLS), 0)
    acc = jnp.zeros((tt, D_MODEL), F32)
    for j in range(D_FF // FFN_COLS):
        conv = []
        for half in range(2):
            c0 = half * D_FF + j * FFN_COLS
            cs = slice(c0, c0 + FFN_COLS)
            u = _dot(xb, wup_ref[:, cs])
            w0, w1, w2 = cw_ref[0:1, cs], cw_ref[1:2, cs], cw_ref[2:3, cs]
            outs = []
            for c in range(nch):
                cur = u[c * CHUNK:(c + 1) * CHUNK, :]
                prev = hst[:, cs] if c == 0 else u[c * CHUNK - 16:c * CHUNK, :]
                e2 = _wrapped_slab(cur[CHUNK - 16:CHUNK - 8, :], prev[0:8, :], 1, sub)
                e1 = _wrapped_slab(cur[CHUNK - 8:CHUNK, :], prev[8:16, :], 1, sub)
                back1 = jnp.concatenate([e1, cur[0:CHUNK - 8, :]], axis=0)
                back2 = jnp.concatenate([e2, e1, cur[0:CHUNK - 16, :]], axis=0)
                outs.append(cb_ref[:, cs] + w2 * cur + w1 * back1 + w0 * back2)
            hst[:, cs] = u[tt - 16:tt, :]
            conv.append(jnp.concatenate(outs, axis=0))
        f = _silu(conv[1]) * conv[0]
        acc = acc + _dot(f.astype(BF16), wdn_ref[j * FFN_COLS:(j + 1) * FFN_COLS, :])
    o_ref[0] = _layer_norm(ALPHA * x_ref[0] + acc, g_ref[...], b_ref[...])
    st_ref[0, 0:1, :] = hst[7:8, :]
    st_ref[0, 1:2, :] = hst[15:16, :]


def _prompt_ffn(x, layer, wup, wdn, cw, cb, g, b, tt):
    bsz, t, d = x.shape
    tile = pl.BlockSpec((1, tt, d), lambda bi, i: (bi, i, 0))
    return pl.pallas_call(
        functools.partial(_prompt_ffn_kernel, tt),
        grid=(bsz, t // tt),
        in_specs=[tile, _layer_resident(wup, layer), _layer_resident(wdn, layer), _resident(cw.shape),
                  _resident(cb.shape), _resident(g.shape), _resident(b.shape)],
        out_specs=[tile, pl.BlockSpec((1, FFN_CONV - 1, 2 * D_FF), lambda bi, i: (bi, 0, 0))],
        out_shape=[jax.ShapeDtypeStruct((bsz, t, d), F32),
                   jax.ShapeDtypeStruct((bsz, FFN_CONV - 1, 2 * D_FF), F32)],
        scratch_shapes=[pltpu.VMEM((16, 2 * D_FF), F32)],
        compiler_params=_cparams(2),
        name="prompt_ffn",
    )(x, wup, wdn, cw, cb, g, b)


MEM_ROWS = MEM_LEN * (XA_HEAD_DIM // LANES) * XA_HEADS


def _flat_cache(c):
    nl, n = c.shape[0], c.shape[1]
    c = c.reshape(nl, n, MEM_LEN, XA_HEADS, XA_HEAD_DIM // LANES, LANES)
    return c.transpose(0, 1, 2, 4, 3, 5).reshape(nl, n, MEM_ROWS, LANES)


def _unflat_cache(c):
    nl, n = c.shape[0], c.shape[1]
    c = c.reshape(nl, n, MEM_LEN, XA_HEAD_DIM // LANES, XA_HEADS, LANES)
    return c.transpose(0, 1, 2, 4, 3, 5).reshape(nl, n, MEM_LEN, XA_HEADS, XA_HEAD_DIM)


def _load_flat_head(ref, idx, hd):
    rows_per_m = MEM_ROWS // MEM_LEN
    return jnp.concatenate([ref[idx, pl.ds(half * XA_HEADS + hd, MEM_LEN, stride=rows_per_m), :]
                            for half in range(XA_HEAD_DIM // LANES)], axis=-1)


def _mem_proj_kernel(tm, x_ref, w_ref, o_ref):
    o = _dot(x_ref[...].astype(BF16), w_ref[...])
    rows_per_m = MEM_ROWS // MEM_LEN
    for hd in range(XA_HEADS):
        for half in range(XA_HEAD_DIM // LANES):
            lo = hd * XA_HEAD_DIM + half * LANES
            o_ref[pl.ds(half * XA_HEADS + hd, tm, stride=rows_per_m), :] = o[:, lo:lo + LANES]


def _mem_proj(mem, w, tm):
    m, d = mem.shape
    nl = w.shape[0]
    rows_per_m = MEM_ROWS // MEM_LEN
    return pl.pallas_call(
        functools.partial(_mem_proj_kernel, tm),
        grid=(nl, m // tm),
        in_specs=[pl.BlockSpec((tm, d), lambda l, i: (i, 0)),
                  pl.BlockSpec((None,) + w.shape[1:], lambda l, i: (l, 0, 0))],
        out_specs=pl.BlockSpec((None, tm * rows_per_m, LANES), lambda l, i: (l, i, 0)),
        out_shape=jax.ShapeDtypeStruct((nl, m * rows_per_m, LANES), F32),
        compiler_params=_cparams(2),
        name="mem_proj",
    )(mem, w)


def _mm_k_kernel(x_ref, w_ref, o_ref):
    @pl.when(pl.program_id(0) == 0)
    def _():
        o_ref[...] = jnp.zeros(o_ref.shape, F32)

    o_ref[...] += _dot(x_ref[...].astype(BF16), w_ref[...])


def _mm_k(x, w, layer, tk=256):
    m, k = x.shape
    n = w.shape[2]
    return pl.pallas_call(
        _mm_k_kernel,
        grid=(k // tk,),
        in_specs=[pl.BlockSpec((m, tk), lambda i: (0, i)), pl.BlockSpec((None, tk, n), lambda i: (layer, i, 0))],
        out_specs=pl.BlockSpec((m, n), lambda i: (0, 0)),
        out_shape=jax.ShapeDtypeStruct((m, n), F32),
        compiler_params=_cparams(1),
        name="mm_k",
    )(x, w)


def _sample_conf_kernel(h_ref, st_ref, cw_ref, cb_ref, g_ref, b_ref, a_ref, nst_ref):
    glu = h_ref[...]
    a = glu[:, :CONF_W] * _sigmoid(glu[:, CONF_W:])
    nk = CONF_KERNEL - 1
    acc = cb_ref[...] + cw_ref[nk:nk + 1, :] * a
    for k in range(nk):
        acc = acc + cw_ref[k:k + 1, :] * st_ref[:, k * CONF_W:(k + 1) * CONF_W]
    a_ref[...] = _silu(_layer_norm(acc, g_ref[...], b_ref[...])).astype(BF16)
    nst_ref[:, 0:(nk - 1) * CONF_W] = st_ref[:, CONF_W:nk * CONF_W]
    nst_ref[:, (nk - 1) * CONF_W:nk * CONF_W] = a


def _sample_conf(h, st, layer, cw, cb, g, b, bb=32):
    n = h.shape[0]
    sw = (CONF_KERNEL - 1) * CONF_W
    return pl.pallas_call(
        _sample_conf_kernel,
        grid=(n // bb,),
        in_specs=[pl.BlockSpec((bb, 2 * CONF_W), lambda i: (i, 0)),
                  pl.BlockSpec((None, bb, sw), lambda i: (layer, i, 0)),
                  _resident(cw.shape), _resident(cb.shape), _resident(g.shape), _resident(b.shape)],
        out_specs=[pl.BlockSpec((bb, CONF_W), lambda i: (i, 0)), pl.BlockSpec((bb, sw), lambda i: (i, 0))],
        out_shape=[jax.ShapeDtypeStruct((n, CONF_W), BF16), jax.ShapeDtypeStruct((n, sw), F32)],
        compiler_params=_cparams(1),
        name="sample_conf",
    )(h, st, cw, cb, g, b)


def _sample_ssd_prep_kernel(xbc_ref, dt_ref, st_ref, sw_ref, sb_ref, dtb_ref, alog_ref, rexp_ref,
                            nst_ref, xs_ref, bc_ref, xdt_t_ref, decay_ref):
    xr = xbc_ref[...]
    nk = SSD_CONV - 1
    acc = sb_ref[...] + sw_ref[nk:nk + 1, :] * xr
    for k in range(nk):
        acc = acc + sw_ref[k:k + 1, :] * st_ref[:, k * XBC_W:(k + 1) * XBC_W]
    xbc = _silu(acc)
    nst_ref[:, 0:(nk - 1) * XBC_W] = st_ref[:, XBC_W:nk * XBC_W]
    nst_ref[:, (nk - 1) * XBC_W:nk * XBC_W] = xr
    xs = xbc[:, :SSD_W]
    xs_ref[...] = xs
    bc_ref[...] = xbc[:, SSD_W:]
    dt = _softplus(dt_ref[...] + dtb_ref[...])
    rexp = rexp_ref[...]
    a_neg = -jnp.exp(alog_ref[...])
    xdt_t_ref[...] = (xs * _dot_f32(dt, rexp)).T
    decay_ref[...] = jnp.exp(dt * a_neg)


def _sample_ssd_prep(h, st, layer, sw, sb, dtb, alog, rexp):
    n = h.shape[0]
    stw = (SSD_CONV - 1) * XBC_W
    full = lambda r, c: pl.BlockSpec((r, c), lambda i: (0, 0))
    return pl.pallas_call(
        _sample_ssd_prep_kernel,
        grid=(1,),
        in_specs=[pl.BlockSpec((n, XBC_W), lambda i: (0, C_XBC // XBC_W)),
                  pl.BlockSpec((n, LANES), lambda i: (0, C_DT // LANES)),
                  pl.BlockSpec((None, n, stw), lambda i: (layer, 0, 0)),
                  full(*sw.shape), full(*sb.shape), full(*dtb.shape), full(*alog.shape), full(*rexp.shape)],
        out_specs=[full(n, stw), full(n, SSD_W), full(n, XBC_W - SSD_W), full(SSD_W, n), full(n, LANES)],
        out_shape=[jax.ShapeDtypeStruct((n, stw), F32), jax.ShapeDtypeStruct((n, SSD_W), F32),
                   jax.ShapeDtypeStruct((n, XBC_W - SSD_W), F32), jax.ShapeDtypeStruct((SSD_W, n), F32),
                   jax.ShapeDtypeStruct((n, LANES), F32)],
        compiler_params=_cparams(1),
        name="sample_ssd_prep",
    )(h, h, st, sw, sb, dtb, alog, rexp)


def _split_bf16(x):
    hi = x.astype(BF16)
    return hi, (x - hi.astype(F32)).astype(BF16)


def _sample_ssd_state_kernel(n, bg, decay_ref, st_ref, xdt_t_ref, b_ref, c_ref, *refs):
    nst_ref, y_t_ref = refs[-2:]
    h = pl.program_id(0)
    lane = lax.broadcasted_iota(jnp.int32, (SSD_HEAD_DIM, n), 1)
    x_hi, x_lo = _split_bf16(xdt_t_ref[...])
    b_hi, b_lo = _split_bf16(b_ref[...])
    b_cat = jnp.concatenate([b_hi, b_lo, b_hi], axis=0)
    c_bf = c_ref[...].astype(BF16)
    zero = jnp.zeros((SSD_HEAD_DIM, n), BF16)
    y_t = jnp.zeros((SSD_HEAD_DIM, n), F32)
    for b0 in range(0, n, bg):
        hi_m = jnp.concatenate([jnp.where(lane == b0 + j, x_hi, zero) for j in range(bg)], axis=0)
        lo_m = jnp.concatenate([jnp.where(lane == b0 + j, x_lo, zero) for j in range(bg)], axis=0)
        upd = _dot(jnp.concatenate([hi_m, hi_m, lo_m], axis=1), b_cat)
        hn = []
        for j in range(bg):
            hj = st_ref[b0 + j] * decay_ref[b0 + j, h] + upd[j * SSD_HEAD_DIM:(j + 1) * SSD_HEAD_DIM, :]
            nst_ref[b0 + j] = hj
            hn.append(hj.astype(BF16))
        yy = lax.dot_general(jnp.concatenate(hn, axis=0), c_bf, (((1,), (1,)), ((), ())),
                             preferred_element_type=F32)
        for j in range(bg):
            y_t = jnp.where(lane == b0 + j, yy[j * SSD_HEAD_DIM:(j + 1) * SSD_HEAD_DIM, :], y_t)
    y_t_ref[...] = y_t


def _sample_ssd_state(st, layer, decay, xdt_t, bc, stacked, bg=8):
    n = st.shape[1]
    hpg = SSD_HEADS // SSD_GROUPS
    col = pl.BlockSpec((SSD_HEAD_DIM, n), lambda h: (h, 0))
    state_block = pl.BlockSpec((None, n, None, SSD_HEAD_DIM, SSD_STATE), lambda h: (layer, 0, h, 0, 0))
    in_specs = [pl.BlockSpec(memory_space=pltpu.SMEM), state_block, col,
                pl.BlockSpec((n, SSD_STATE), lambda h: (0, h // hpg)),
                pl.BlockSpec((n, SSD_STATE), lambda h: (0, SSD_GROUPS + h // hpg))]
    args = [decay, st, xdt_t, bc, bc]
    aliases = {}
    if stacked is not None:
        in_specs.append(pl.BlockSpec(memory_space=pl.ANY))
        args.append(stacked)
        aliases = {len(args) - 1: 0}
    return pl.pallas_call(
        functools.partial(_sample_ssd_state_kernel, n, bg),
        grid=(SSD_HEADS,),
        in_specs=in_specs,
        out_specs=[state_block, col],
        out_shape=[jax.ShapeDtypeStruct(st.shape, F32), jax.ShapeDtypeStruct((SSD_W, n), F32)],
        input_output_aliases=aliases,
        compiler_params=_cparams(1),
        name="sample_ssd_state",
    )(*args)


def _sample_mix_out_kernel(y_t_ref, xs_ref, z_ref, a_ref, x_ref, wo_ref, dx_ref, ng_ref, g_ref, b_ref, o_ref):
    y = y_t_ref[...].T + dx_ref[...] * xs_ref[...]
    y = y * _silu(z_ref[...])
    ms = jnp.mean(y * y, -1, keepdims=True)
    y = y * lax.rsqrt(ms + LN_EPS) * ng_ref[...]
    acc = ALPHA * x_ref[...] + _dot(a_ref[...], wo_ref[0:CONF_W, :]) + _dot(y.astype(BF16), wo_ref[CONF_W:, :])
    o_ref[...] = _layer_norm(acc, g_ref[...], b_ref[...])


def _sample_mix_out(y_t, xs, h, a, x, wo, layer, dx, ng, g, b):
    n = x.shape[0]
    full = lambda arr: pl.BlockSpec(arr.shape, lambda i: (0,) * arr.ndim)
    return pl.pallas_call(
        _sample_mix_out_kernel,
        grid=(1,),
        in_specs=[full(y_t), full(xs), pl.BlockSpec((n, SSD_W), lambda i: (0, C_Z // SSD_W)), full(a),
                  full(x), _layer_resident(wo, layer), full(dx), full(ng), full(g), full(b)],
        out_specs=full(x),
        out_shape=jax.ShapeDtypeStruct(x.shape, F32),
        compiler_params=_cparams(1),
        name="sample_mix_out",
    )(y_t, xs, h, a, x, wo, dx, ng, g, b)


def _sample_attn_kernel(bb, q_ref, k_ref, v_ref, o_ref):
    nhalf = XA_HEAD_DIM // LANES
    rows_per_m = nhalf * XA_HEADS
    for j in range(bb):
        outs = []
        for hd in range(XA_HEADS):
            prod = None
            for half in range(nhalf):
                kk = k_ref[j, pl.ds(half * XA_HEADS + hd, MEM_LEN, stride=rows_per_m), :]
                lo = hd * XA_HEAD_DIM + half * LANES
                part = kk * q_ref[j, :, lo:lo + LANES]
                prod = part if prod is None else prod + part
            s = jnp.sum(prod, axis=-1, keepdims=True) * (XA_HEAD_DIM ** -0.5)
            e = jnp.exp(s - jnp.max(s, axis=0, keepdims=True))
            p = e / jnp.sum(e, axis=0, keepdims=True)
            for half in range(nhalf):
                vv = v_ref[j, pl.ds(half * XA_HEADS + hd, MEM_LEN, stride=rows_per_m), :]
                outs.append(jnp.sum(p * vv, axis=0, keepdims=True))
        o_ref[j] = jnp.concatenate(outs, axis=-1)


def _sample_attn(q, mk, mv, layer, bb=4):
    n, d = q.shape
    row = pl.BlockSpec((bb, 1, d), lambda i: (i, 0, 0))
    mem = pl.BlockSpec((None, bb, mk.shape[2], LANES), lambda i: (layer, i, 0, 0))
    o = pl.pallas_call(
        functools.partial(_sample_attn_kernel, bb),
        grid=(n // bb,),
        in_specs=[row, mem, mem],
        out_specs=row,
        out_shape=jax.ShapeDtypeStruct((n, 1, d), F32),
        compiler_params=_cparams(1),
        name="sample_attn",
    )(q.reshape(n, 1, d), mk, mv)
    return o.reshape(n, d)


def _sample_ffn_kernel(u_ref, st_ref, cw_ref, cb_ref, x_ref, wdn_ref, g_ref, b_ref, o_ref, nst_ref):
    w2 = 2 * D_FF
    u = u_ref[...]
    cv = cb_ref[...] + cw_ref[0:1, :] * st_ref[:, 0:w2] + cw_ref[1:2, :] * st_ref[:, w2:2 * w2] \
        + cw_ref[2:3, :] * u
    nst_ref[:, 0:w2] = st_ref[:, w2:2 * w2]
    nst_ref[:, w2:2 * w2] = u
    f = _silu(cv[:, D_FF:]) * cv[:, :D_FF]
    acc = ALPHA * x_ref[...] + _dot(f.astype(BF16), wdn_ref[...])
    o_ref[...] = _layer_norm(acc, g_ref[...], b_ref[...])


def _sample_ffn(u, st, layer, cw, cb, x, wdn, g, b):
    n = x.shape[0]
    stw = (FFN_CONV - 1) * 2 * D_FF
    one = lambda arr: pl.BlockSpec(arr.shape, lambda i: (0,) * arr.ndim, pipeline_mode=pl.Buffered(1))
    return pl.pallas_call(
        _sample_ffn_kernel,
        grid=(1,),
        in_specs=[one(u), pl.BlockSpec((None, n, stw), lambda i: (layer, 0, 0), pipeline_mode=pl.Buffered(1)),
                  one(cw), one(cb), one(x), _layer_resident(wdn, layer), one(g), one(b)],
        out_specs=[pl.BlockSpec(x.shape, lambda i: (0, 0)), pl.BlockSpec((n, stw), lambda i: (0, 0))],
        out_shape=[jax.ShapeDtypeStruct(x.shape, F32), jax.ShapeDtypeStruct((n, stw), F32)],
        compiler_params=_cparams(1),
        name="sample_ffn",
    )(u, st, cw, cb, x, wdn, g, b)


def kernel(x_prompt, x_sample, cache_mem_k, cache_mem_v, state_conf_conv, state_ssd_conv, state_ssd,
           state_ffn_conv, mem_prompt, w_in, conf_conv_w, conf_conv_b, conf_ln_g, conf_ln_b, ssd_conv_w,
           ssd_conv_b, ssd_dt_bias, ssd_a_log, ssd_d, ssd_norm_g, w_out, ln_mix_g, ln_mix_b, xa_wq, xa_wk,
           xa_wv, xa_wo, ln_xa_g, ln_xa_b, ffn_w_up, ffn_conv_w, ffn_conv_b, ffn_w_down, ln_ffn_g, ln_ffn_b):
    bp, t, d = x_prompt.shape
    ns = x_sample.shape[0]
    nl = DEPTH
    tt_mix, tt = min(MIXER_TILE, t), min(ROW_TILE, t)

    row = lambda p, l: p[l][None, :]
    pad_heads = lambda p, l: jnp.pad(p[l], (0, LANES - SSD_HEADS))[None, :]
    rexp = jnp.pad(jnp.repeat(jnp.eye(SSD_HEADS, dtype=F32), SSD_HEAD_DIM, axis=1),
                   ((0, LANES - SSD_HEADS), (0, 0)))

    win = jnp.pad(w_in, ((0, 0), (0, 0), (0, D_IN_PAD - w_in.shape[2]))).astype(BF16)
    wout, wq, wo = w_out.astype(BF16), xa_wq.astype(BF16), xa_wo.astype(BF16)
    wup, wdn = ffn_w_up.astype(BF16), ffn_w_down.astype(BF16)

    mem2d = mem_prompt.reshape(bp * MEM_LEN, d)
    mk_p = _mem_proj(mem2d, xa_wk.astype(BF16), 512).reshape(nl, bp, MEM_ROWS, LANES)
    mv_p = _mem_proj(mem2d, xa_wv.astype(BF16), 512).reshape(nl, bp, MEM_ROWS, LANES)
    mk_s, mv_s = _flat_cache(cache_mem_k), _flat_cache(cache_mem_v)
    conf_s = state_conf_conv.reshape(nl, ns, (CONF_KERNEL - 1) * CONF_W)
    ssdc_s = state_ssd_conv.reshape(nl, ns, (SSD_CONV - 1) * XBC_W)
    ffn_s = state_ffn_conv.reshape(nl, ns, (FFN_CONV - 1) * 2 * D_FF)

    xp = _permute_rows(x_prompt, inverse=False)
    xs_ = x_sample.reshape(ns, d)
    st_p = ([], [], [], [])
    st_s = ([], [], [])
    nst = None
    for l in range(nl):
        cw, cb = conf_conv_w[l], row(conf_conv_b, l)
        cg, cbeta = row(conf_ln_g, l), row(conf_ln_b, l)
        sw, sb = ssd_conv_w[l], row(ssd_conv_b, l)
        dtb, alog = pad_heads(ssd_dt_bias, l), pad_heads(ssd_a_log, l)
        dx = jnp.repeat(ssd_d[l], SSD_HEAD_DIM)[None, :]
        ng = row(ssd_norm_g, l)
        g1, b1 = row(ln_mix_g, l), row(ln_mix_b, l)
        g2, b2 = row(ln_xa_g, l), row(ln_xa_b, l)
        g3, b3 = row(ln_ffn_g, l), row(ln_ffn_b, l)
        fcw, fcb = ffn_conv_w[l], row(ffn_conv_b, l)

        x1, cst, sst, hst = _prompt_mixer(xp, l, win, wout, cw, cb, cg, cbeta, sw, sb, dtb, alog, dx, ng, g1, b1,
                                          tt_mix)
        x2 = _prompt_xattn(x1, mk_p, mv_p, l, wq, wo, g2, b2, tt)
        xp, fst = _prompt_ffn(x2, l, wup, wdn, fcw, fcb, g3, b3, tt)
        for lst, s in zip(st_p, (cst, sst, hst, fst)):
            lst.append(s)

        h = _mm_k(xs_, win, l)
        a_s, ncs = _sample_conf(h, conf_s, l, cw, cb, cg, cbeta)
        nss, xs1, bc, xdt_t, decay = _sample_ssd_prep(h, ssdc_s, l, sw, sb, dtb, alog, rexp)
        nst, y_t = _sample_ssd_state(state_ssd, l, decay[:, :SSD_HEADS], xdt_t, bc, nst)
        x1s = _sample_mix_out(y_t, xs1, h, a_s, xs_, wout, l, dx, ng, g1, b1)
        qs = _mm_k(x1s, wq, l)
        os_ = _sample_attn(qs, mk_s, mv_s, l)
        x2s = _proj_ln([os_], [wo], l, x1s, g2, b2, ns)
        us = _mm_k(x2s, wup, l)
        xs_, nfs = _sample_ffn(us, ffn_s, l, fcw, fcb, x2s, wdn, g3, b3)
        for lst, s in zip(st_s, (ncs, nss, nfs)):
            lst.append(s)

    st_p = [jnp.stack(s) for s in st_p]
    st_s = [jnp.stack(s) for s in st_s]
    return (_permute_rows(xp, inverse=True), xs_.reshape(ns, 1, d), st_p[0], st_p[1], st_p[2], st_p[3],
            _unflat_cache(mk_p), _unflat_cache(mv_p),
            st_s[0].reshape(nl, ns, CONF_KERNEL - 1, CONF_W), st_s[1].reshape(nl, ns, SSD_CONV - 1, XBC_W),
            nst, st_s[2].reshape(nl, ns, FFN_CONV - 1, 2 * D_FF))
```

```python
import functools

import jax
import jax.numpy as jnp
from jax import lax
from jax.experimental import pallas as pl
from jax.experimental.pallas import tpu as pltpu

F32 = jnp.float32
BF16 = jnp.bfloat16
HIGHEST = lax.Precision.HIGHEST

D_MODEL = 1024
DEPTH = 4
CONF_W = 1024
CONF_KERNEL = 31
SSD_HEAD_DIM = 64
SSD_HEADS = 16
SSD_W = 1024
SSD_GROUPS = 2
SSD_STATE = 128
SSD_CONV = 4
SSD_CHUNK = 128
XBC_W = SSD_W + 2 * SSD_GROUPS * SSD_STATE
MEM_LEN = 256
XA_HEADS = 4
XA_HEAD_DIM = 256
D_FF = 2816
FFN_CONV = 3
ALPHA = (2.0 * DEPTH) ** 0.25
LN_EPS = 1e-5

LANES = 128
C_GLU, C_Z, C_XBC, C_DT = 0, 2 * CONF_W, 2 * CONF_W + SSD_W, 2 * CONF_W + SSD_W + XBC_W
D_IN_PAD = C_DT + LANES
PROJ_COLS = 256
MIXER_TILE = 256
XATTN_TILE = 1024
FFN_TILE = 512
FFN_COLS = 256
VMEM_LIMIT = 56 * 1024 * 1024


def _cparams(n_grid_dims):
    return pltpu.CompilerParams(dimension_semantics=("arbitrary",) * n_grid_dims,
                                vmem_limit_bytes=VMEM_LIMIT)


def _resident(shape):
    nd = len(shape)
    return pl.BlockSpec(shape, lambda *_: (0,) * nd, pipeline_mode=pl.Buffered(1))


def _layer_resident(w, layer):
    return pl.BlockSpec((None,) + w.shape[1:], lambda *_: (layer, 0, 0), pipeline_mode=pl.Buffered(1))


def _sigmoid(x):
    return jax.nn.sigmoid(x)


def _silu(x):
    return x * _sigmoid(x)


def _softplus(x):
    return jnp.maximum(x, 0.0) + jnp.log(1.0 + jnp.exp(-jnp.abs(x)))


def _layer_norm(x, g, b):
    mu = jnp.mean(x, -1, keepdims=True)
    xc = x - mu
    var = jnp.mean(xc * xc, -1, keepdims=True)
    return xc * lax.rsqrt(var + LN_EPS) * g + b


def _dot(a, b):
    return jnp.dot(a, b, preferred_element_type=F32)


def _dot_f32(a, b):
    return jnp.dot(a, b, preferred_element_type=F32, precision=HIGHEST)


CHUNK = SSD_CHUNK
SLABS = CHUNK // 8


def _permuted_row(token):
    return 8 * (token % SLABS) + token // SLABS


def _wrapped_slab(cur, prev, shift, sub):
    return jnp.where(sub >= shift, pltpu.roll(cur, shift, 0), pltpu.roll(prev, shift, 0))


def _fill_history(src, base, eseq, nk, width):
    sub = lax.broadcasted_iota(jnp.int32, (8, width), 0)
    for i in range(SLABS):
        for shift in (1, 2):
            p = i - SLABS * shift + (nk - 1)
            if 0 <= p < nk - 1:
                cur = src[base + 8 * i:base + 8 * (i + 1), :]
                prev = src[base - CHUNK + 8 * i:base - CHUNK + 8 * (i + 1), :]
                eseq[8 * p:8 * (p + 1), :] = _wrapped_slab(cur, prev, shift, sub)
    eseq[8 * (nk - 1):8 * (nk - 1) + CHUNK, :] = src[base:base + CHUNK, :]


def _conv_from_history(eseq, w_ref, nk, lo, rows):
    r0, nr = rows
    acc = w_ref[0:1, lo:lo + LANES] * eseq[r0:r0 + nr, lo:lo + LANES]
    for k in range(1, nk):
        acc = acc + w_ref[k:k + 1, lo:lo + LANES] * eseq[8 * k + r0:8 * k + r0 + nr, lo:lo + LANES]
    return acc


def _permute_rows_kernel(tt, inverse, *refs):
    in_refs, o_ref = refs[:-1], refs[-1]
    for c in range(tt // CHUNK):
        for lt, x_ref in enumerate(in_refs):
            ls = slice(lt * LANES, (lt + 1) * LANES)
            if not inverse:
                for i in range(SLABS):
                    o_ref[0, c * CHUNK + 8 * i:c * CHUNK + 8 * (i + 1), ls] = \
                        x_ref[0, pl.ds(c * CHUNK + i, 8, stride=SLABS), :]
            else:
                for s in range(8):
                    for i0 in range(0, SLABS, 8):
                        tok = c * CHUNK + s * SLABS + i0
                        o_ref[0, tok:tok + 8, ls] = x_ref[0, pl.ds(c * CHUNK + 8 * i0 + s, 8, stride=8), :]


def _permute_rows(x, inverse, tt=512):
    bsz, t, w = x.shape
    tt = min(tt, t)
    lane_tile = lambda lt: pl.BlockSpec((1, tt, LANES), lambda bi, i: (bi, i, lt))
    return pl.pallas_call(
        functools.partial(_permute_rows_kernel, tt, inverse),
        grid=(bsz, t // tt),
        in_specs=[lane_tile(lt) for lt in range(w // LANES)],
        out_specs=pl.BlockSpec((1, tt, w), lambda bi, i: (bi, i, 0)),
        out_shape=jax.ShapeDtypeStruct(x.shape, x.dtype),
        compiler_params=_cparams(2),
        name="permute_rows",
    )(*([x] * (w // LANES)))


def _permuted_time(n):
    r = lax.broadcasted_iota(jnp.int32, (CHUNK, CHUNK), n)
    return (r & 7) * SLABS + (r >> 3)


def _prompt_mixer_kernel(tt, x_ref, win_ref, wout_ref, cw_ref, cb_ref, cg_ref, cbeta_ref, sw_ref, sb_ref,
                         dtb_ref, alog_ref, dx_ref, ng_ref, g_ref, b_ref,
                         o_ref, cst_ref, sst_ref, hst_ref,
                         abuf, xbuf, zbuf, dtbuf, eseq, eseq2, cvbuf, xact, ygbuf, abf, ybf, ht):
    t = pl.program_id(1)
    nch = tt // CHUNK

    @pl.when(t == 0)
    def _():
        abuf[0:CHUNK, :] = jnp.zeros((CHUNK, CONF_W), F32)
        xbuf[0:CHUNK, :] = jnp.zeros((CHUNK, XBC_W), F32)
        ht[...] = jnp.zeros(ht.shape, F32)

    xb = x_ref[0].astype(BF16)
    nb = PROJ_COLS
    for j in range(CONF_W // nb):
        blk = _dot(xb, win_ref[:, C_GLU + 2 * nb * j:C_GLU + 2 * nb * (j + 1)])
        abuf[CHUNK:CHUNK + tt, j * nb:(j + 1) * nb] = blk[:, :nb] * _sigmoid(blk[:, nb:])
    for j in range(SSD_W // nb):
        zbuf[:, j * nb:(j + 1) * nb] = _dot(xb, win_ref[:, C_Z + j * nb:C_Z + (j + 1) * nb])
    for j in range(XBC_W // nb):
        xbuf[CHUNK:CHUNK + tt, j * nb:(j + 1) * nb] = _dot(xb, win_ref[:, C_XBC + j * nb:C_XBC + (j + 1) * nb])
    dtbuf[...] = _softplus(_dot(xb, win_ref[:, C_DT:C_DT + LANES]) + dtb_ref[...])

    tl, ts = _permuted_time(0), _permuted_time(1)
    causal = tl >= ts
    causal_f = causal.astype(F32)
    a_neg = -jnp.exp(alog_ref[...])
    first_head = lax.broadcasted_iota(jnp.int32, (CHUNK, LANES), 1) < SSD_HEAD_DIM
    rb = 32

    for c in range(nch):
        base = CHUNK * (c + 1)
        r0 = CHUNK * c
        _fill_history(abuf, base, eseq, CONF_KERNEL, CONF_W)
        for lt in range(CONF_W // LANES):
            for r in range(CHUNK // rb):
                cvbuf[r * rb:(r + 1) * rb, lt * LANES:(lt + 1) * LANES] = _conv_from_history(
                    eseq, cw_ref, CONF_KERNEL, lt * LANES, (r * rb, rb))
        y = _layer_norm(cvbuf[...] + cb_ref[...], cg_ref[...], cbeta_ref[...])
        abf[r0:r0 + CHUNK, :] = _silu(y).astype(BF16)

        _fill_history(xbuf, base, eseq2, SSD_CONV, XBC_W)
        for lt in range(XBC_W // LANES):
            lo = lt * LANES
            acc = _conv_from_history(eseq2, sw_ref, SSD_CONV, lo, (0, CHUNK))
            xact[:, lo:lo + LANES] = _silu(acc + sb_ref[:, lo:lo + LANES])

        dt = dtbuf[r0:r0 + CHUNK, :]
        acum = _dot_f32(causal_f, dt * a_neg)
        act = acum.T
        ss = jnp.zeros((CHUNK, 1), F32)
        for g in range(SSD_GROUPS):
            bm = xact[:, SSD_W + g * SSD_STATE:SSD_W + (g + 1) * SSD_STATE]
            cmb = xact[:, SSD_W + (SSD_GROUPS + g) * SSD_STATE:SSD_W + (SSD_GROUPS + g + 1) * SSD_STATE].astype(BF16)
            cb = lax.dot_general(cmb, bm.astype(BF16), (((1,), (1,)), ((), ())), preferred_element_type=F32)
            bmt = bm.T.astype(BF16)
            for jj in range(SSD_W // SSD_GROUPS // LANES):
                lo = (g * (SSD_W // SSD_GROUPS // LANES) + jj) * LANES
                h0 = lo // SSD_HEAD_DIM
                dtl = jnp.where(first_head, dt[:, h0:h0 + 1], dt[:, h0 + 1:h0 + 2])
                acl = jnp.where(first_head, acum[:, h0:h0 + 1], acum[:, h0 + 1:h0 + 2])
                xs = xact[:, lo:lo + LANES]
                dtx = xs * dtl
                dtxb = dtx.astype(BF16)
                a_last = acl[CHUNK - 1:CHUNK, :]
                hprev = ht[:, lo:lo + LANES]
                y_off = _dot(cmb, hprev.astype(BF16)) * jnp.exp(acl)
                yd = []
                for h in (h0, h0 + 1):
                    decay = jnp.exp(jnp.where(causal, acum[:, h:h + 1] - act[h:h + 1, :], -1e30))
                    yd.append(_dot((cb * decay).astype(BF16), dtxb))
                y = jnp.where(first_head, yd[0], yd[1]) + y_off + dx_ref[:, lo:lo + LANES] * xs
                yg = y * _silu(zbuf[r0:r0 + CHUNK, lo:lo + LANES])
                ygbuf[:, lo:lo + LANES] = yg
                ss = ss + jnp.sum(yg * yg, -1, keepdims=True)
                ht[:, lo:lo + LANES] = hprev * jnp.exp(a_last) + _dot(bmt, (dtx * jnp.exp(a_last - acl)).astype(BF16))
        inv = lax.rsqrt(ss * (1.0 / SSD_W) + LN_EPS)
        ybf[r0:r0 + CHUNK, :] = (ygbuf[...] * inv * ng_ref[...]).astype(BF16)

    mix = _dot(abf[...], wout_ref[0:CONF_W, :]) + _dot(ybf[...], wout_ref[CONF_W:, :])
    o_ref[0] = _layer_norm(ALPHA * x_ref[0] + mix, g_ref[...], b_ref[...])

    @pl.when(t == pl.num_programs(1) - 1)
    def _():
        for r in range(CONF_KERNEL - 1):
            src_row = tt + _permuted_row(CHUNK - (CONF_KERNEL - 1) + r)
            cst_ref[0, r:r + 1, :] = abuf[src_row:src_row + 1, :]
        for r in range(SSD_CONV - 1):
            src_row = tt + _permuted_row(CHUNK - (SSD_CONV - 1) + r)
            sst_ref[0, r:r + 1, :] = xbuf[src_row:src_row + 1, :]
        hst_ref[0] = ht[...].T.reshape(SSD_HEADS, SSD_HEAD_DIM, SSD_STATE)

    abuf[0:CHUNK, :] = abuf[tt:tt + CHUNK, :]
    xbuf[0:CHUNK, :] = xbuf[tt:tt + CHUNK, :]


def _prompt_mixer(x, layer, win, wout, cw, cb, cg, cbeta, sw, sb, dtb, alog, dx, ng, g, b, tt):
    bsz, t, d = x.shape
    tile = pl.BlockSpec((1, tt, d), lambda bi, i: (bi, i, 0))
    per_b = lambda *s: pl.BlockSpec((1,) + s, lambda bi, i: (bi,) + (0,) * len(s))
    consts = (cw, cb, cg, cbeta, sw, sb, dtb, alog, dx, ng, g, b)
    return pl.pallas_call(
        functools.partial(_prompt_mixer_kernel, tt),
        grid=(bsz, t // tt),
        in_specs=[tile, _layer_resident(win, layer), _layer_resident(wout, layer)]
        + [_resident(a.shape) for a in consts],
        out_specs=[tile, per_b(CONF_KERNEL - 1, CONF_W), per_b(SSD_CONV - 1, XBC_W),
                   per_b(SSD_HEADS, SSD_HEAD_DIM, SSD_STATE)],
        out_shape=[jax.ShapeDtypeStruct((bsz, t, d), F32),
                   jax.ShapeDtypeStruct((bsz, CONF_KERNEL - 1, CONF_W), F32),
                   jax.ShapeDtypeStruct((bsz, SSD_CONV - 1, XBC_W), F32),
                   jax.ShapeDtypeStruct((bsz, SSD_HEADS, SSD_HEAD_DIM, SSD_STATE), F32)],
        scratch_shapes=[pltpu.VMEM((CHUNK + tt, CONF_W), F32),
                        pltpu.VMEM((CHUNK + tt, XBC_W), F32),
                        pltpu.VMEM((tt, SSD_W), F32),
                        pltpu.VMEM((tt, LANES), F32),
                        pltpu.VMEM((8 * (CONF_KERNEL - 1) + CHUNK, CONF_W), F32),
                        pltpu.VMEM((8 * (SSD_CONV - 1) + CHUNK, XBC_W), F32),
                        pltpu.VMEM((CHUNK, CONF_W), F32),
                        pltpu.VMEM((CHUNK, XBC_W), F32),
                        pltpu.VMEM((CHUNK, SSD_W), F32),
                        pltpu.VMEM((tt, CONF_W), BF16),
                        pltpu.VMEM((tt, SSD_W), BF16),
                        pltpu.VMEM((SSD_STATE, SSD_W), F32)],
        compiler_params=_cparams(2),
        name="prompt_mixer",
    )(x, win, wout, *consts)


def _proj_ln_kernel(n_in, *refs):
    in_refs, w_refs = refs[:n_in], refs[n_in:2 * n_in]
    x_ref, g_ref, b_ref, o_ref = refs[2 * n_in:]
    acc = ALPHA * x_ref[...]
    for a_ref, w_ref in zip(in_refs, w_refs):
        acc = acc + _dot(a_ref[...].astype(BF16), w_ref[...])
    o_ref[...] = _layer_norm(acc, g_ref[...], b_ref[...])


def _proj_ln(ins, ws, layer, resid, g, b, tm):
    m, d = resid.shape
    n_in = len(ins)
    rows = lambda a: pl.BlockSpec((tm, a.shape[1]), lambda i: (i, 0))
    return pl.pallas_call(
        functools.partial(_proj_ln_kernel, n_in),
        grid=(m // tm,),
        in_specs=[rows(a) for a in ins] + [_layer_resident(w, layer) for w in ws]
        + [rows(resid), _resident(g.shape), _resident(b.shape)],
        out_specs=rows(resid),
        out_shape=jax.ShapeDtypeStruct((m, d), F32),
        compiler_params=_cparams(1),
        name="proj_ln",
    )(*ins, *ws, resid, g, b)


def _prompt_xattn_kernel(x_ref, k_ref, v_ref, wq_ref, wo_ref, g_ref, b_ref, o_ref):
    x = x_ref[0]
    q = _dot(x.astype(BF16), wq_ref[...])
    acc = ALPHA * x
    for hd in range(XA_HEADS):
        sl = slice(hd * XA_HEAD_DIM, (hd + 1) * XA_HEAD_DIM)
        qh = q[:, sl].astype(BF16)
        kh = _load_flat_head(k_ref, 0, hd).astype(BF16)
        s = lax.dot_general(qh, kh, (((1,), (1,)), ((), ())), preferred_element_type=F32)
        s = s * (XA_HEAD_DIM ** -0.5)
        e = jnp.exp(s - jnp.max(s, -1, keepdims=True))
        p = e / jnp.sum(e, -1, keepdims=True)
        oh = _dot(p.astype(BF16), _load_flat_head(v_ref, 0, hd).astype(BF16))
        acc = acc + _dot(oh.astype(BF16), wo_ref[sl, :])
    o_ref[0] = _layer_norm(acc, g_ref[...], b_ref[...])


def _prompt_xattn(x, mk, mv, layer, wq, wo, g, b, tt):
    bsz, t, d = x.shape
    tile = pl.BlockSpec((1, tt, d), lambda bi, i: (bi, i, 0))
    mem = pl.BlockSpec((None, 1, MEM_ROWS, LANES), lambda bi, i: (layer, bi, 0, 0))
    return pl.pallas_call(
        _prompt_xattn_kernel,
        grid=(bsz, t // tt),
        in_specs=[tile, mem, mem, _layer_resident(wq, layer), _layer_resident(wo, layer), _resident(g.shape),
                  _resident(b.shape)],
        out_specs=tile,
        out_shape=jax.ShapeDtypeStruct((bsz, t, d), F32),
        compiler_params=_cparams(2),
        name="prompt_xattn",
    )(x, mk, mv, wq, wo, g, b)


def _prompt_ffn_kernel(tt, x_ref, wup_ref, wdn_ref, cw_ref, cb_ref, g_ref, b_ref, o_ref, st_ref, hst):
    t = pl.program_id(1)
    nch = tt // CHUNK

    @pl.when(t == 0)
    def _():
        hst[...] = jnp.zeros(hst.shape, F32)

    xb = x_ref[0].astype(BF16)
    sub = lax.broadcasted_iota(jnp.int32, (8, FFN_COLS), 0)
    acc = jnp.zeros((tt, D_MODEL), F32)
    for j in range(D_FF // FFN_COLS):
        conv = []
        for half in range(2):
            c0 = half * D_FF + j * FFN_COLS
            cs = slice(c0, c0 + FFN_COLS)
            u = _dot(xb, wup_ref[:, cs])
            w0, w1, w2 = cw_ref[0:1, cs], cw_ref[1:2, cs], cw_ref[2:3, cs]
            outs = []
            for c in range(nch):
                cur = u[c * CHUNK:(c + 1) * CHUNK, :]
                prev = hst[:, cs] if c == 0 else u[c * CHUNK - 16:c * CHUNK, :]
                e2 = _wrapped_slab(cur[CHUNK - 16:CHUNK - 8, :], prev[0:8, :], 1, sub)
                e1 = _wrapped_slab(cur[CHUNK - 8:CHUNK, :], prev[8:16, :], 1, sub)
                back1 = jnp.concatenate([e1, cur[0:CHUNK - 8, :]], axis=0)
                back2 = jnp.concatenate([e2, e1, cur[0:CHUNK - 16, :]], axis=0)
                outs.append(cb_ref[:, cs] + w2 * cur + w1 * back1 + w0 * back2)
            hst[:, cs] = u[tt - 16:tt, :]
            conv.append(jnp.concatenate(outs, axis=0))
        f = _silu(conv[1]) * conv[0]
        acc = acc + _dot(f.astype(BF16), wdn_ref[j * FFN_COLS:(j + 1) * FFN_COLS, :])
    o_ref[0] = _layer_norm(ALPHA * x_ref[0] + acc, g_ref[...], b_ref[...])
    st_ref[0, 0:1, :] = hst[7:8, :]
    st_ref[0, 1:2, :] = hst[15:16, :]


def _prompt_ffn(x, layer, wup, wdn, cw, cb, g, b, tt):
    bsz, t, d = x.shape
    tile = pl.BlockSpec((1, tt, d), lambda bi, i: (bi, i, 0))
    return pl.pallas_call(
        functools.partial(_prompt_ffn_kernel, tt),
        grid=(bsz, t // tt),
        in_specs=[tile, _layer_resident(wup, layer), _layer_resident(wdn, layer), _resident(cw.shape),
                  _resident(cb.shape), _resident(g.shape), _resident(b.shape)],
        out_specs=[tile, pl.BlockSpec((1, FFN_CONV - 1, 2 * D_FF), lambda bi, i: (bi, 0, 0))],
        out_shape=[jax.ShapeDtypeStruct((bsz, t, d), F32),
                   jax.ShapeDtypeStruct((bsz, FFN_CONV - 1, 2 * D_FF), F32)],
        scratch_shapes=[pltpu.VMEM((16, 2 * D_FF), F32)],
        compiler_params=_cparams(2),
        name="prompt_ffn",
    )(x, wup, wdn, cw, cb, g, b)


MEM_ROWS = MEM_LEN * (XA_HEAD_DIM // LANES) * XA_HEADS


def _flat_cache(c):
    nl, n = c.shape[0], c.shape[1]
    c = c.reshape(nl, n, MEM_LEN, XA_HEADS, XA_HEAD_DIM // LANES, LANES)
    return c.transpose(0, 1, 2, 4, 3, 5).reshape(nl, n, MEM_ROWS, LANES)


def _unflat_cache(c):
    nl, n = c.shape[0], c.shape[1]
    c = c.reshape(nl, n, MEM_LEN, XA_HEAD_DIM // LANES, XA_HEADS, LANES)
    return c.transpose(0, 1, 2, 4, 3, 5).reshape(nl, n, MEM_LEN, XA_HEADS, XA_HEAD_DIM)


def _load_flat_head(ref, idx, hd):
    rows_per_m = MEM_ROWS // MEM_LEN
    return jnp.concatenate([ref[idx, pl.ds(half * XA_HEADS + hd, MEM_LEN, stride=rows_per_m), :]
                            for half in range(XA_HEAD_DIM // LANES)], axis=-1)


def _mem_proj_kernel(tm, x_ref, w_ref, o_ref):
    o = _dot(x_ref[...].astype(BF16), w_ref[...])
    rows_per_m = MEM_ROWS // MEM_LEN
    for hd in range(XA_HEADS):
        for half in range(XA_HEAD_DIM // LANES):
            lo = hd * XA_HEAD_DIM + half * LANES
            o_ref[pl.ds(half * XA_HEADS + hd, tm, stride=rows_per_m), :] = o[:, lo:lo + LANES]


def _mem_proj(mem, w, tm):
    m, d = mem.shape
    nl = w.shape[0]
    rows_per_m = MEM_ROWS // MEM_LEN
    return pl.pallas_call(
        functools.partial(_mem_proj_kernel, tm),
        grid=(nl, m // tm),
        in_specs=[pl.BlockSpec((tm, d), lambda l, i: (i, 0)),
                  pl.BlockSpec((None,) + w.shape[1:], lambda l, i: (l, 0, 0))],
        out_specs=pl.BlockSpec((None, tm * rows_per_m, LANES), lambda l, i: (l, i, 0)),
        out_shape=jax.ShapeDtypeStruct((nl, m * rows_per_m, LANES), F32),
        compiler_params=_cparams(2),
        name="mem_proj",
    )(mem, w)


def _mm_k_kernel(x_ref, w_ref, o_ref):
    @pl.when(pl.program_id(0) == 0)
    def _():
        o_ref[...] = jnp.zeros(o_ref.shape, F32)

    o_ref[...] += _dot(x_ref[...].astype(BF16), w_ref[...])


def _mm_k(x, w, layer, tk=256):
    m, k = x.shape
    n = w.shape[2]
    return pl.pallas_call(
        _mm_k_kernel,
        grid=(k // tk,),
        in_specs=[pl.BlockSpec((m, tk), lambda i: (0, i)), pl.BlockSpec((None, tk, n), lambda i: (layer, i, 0))],
        out_specs=pl.BlockSpec((m, n), lambda i: (0, 0)),
        out_shape=jax.ShapeDtypeStruct((m, n), F32),
        compiler_params=_cparams(1),
        name="mm_k",
    )(x, w)


def _sample_conf_kernel(h_ref, st_ref, cw_ref, cb_ref, g_ref, b_ref, a_ref, nst_ref):
    nb = PROJ_COLS
    a = jnp.concatenate([h_ref[:, 2 * nb * j:2 * nb * j + nb] * _sigmoid(h_ref[:, 2 * nb * j + nb:2 * nb * (j + 1)])
                         for j in range(CONF_W // nb)], axis=-1)
    nk = CONF_KERNEL - 1
    acc = cb_ref[...] + cw_ref[nk:nk + 1, :] * a
    for k in range(nk):
        acc = acc + cw_ref[k:k + 1, :] * st_ref[:, k * CONF_W:(k + 1) * CONF_W]
    a_ref[...] = _silu(_layer_norm(acc, g_ref[...], b_ref[...])).astype(BF16)
    nst_ref[:, 0:(nk - 1) * CONF_W] = st_ref[:, CONF_W:nk * CONF_W]
    nst_ref[:, (nk - 1) * CONF_W:nk * CONF_W] = a


def _sample_conf(h, st, layer, cw, cb, g, b, bb=32):
    n = h.shape[0]
    sw = (CONF_KERNEL - 1) * CONF_W
    return pl.pallas_call(
        _sample_conf_kernel,
        grid=(n // bb,),
        in_specs=[pl.BlockSpec((bb, 2 * CONF_W), lambda i: (i, 0)),
                  pl.BlockSpec((None, bb, sw), lambda i: (layer, i, 0)),
                  _resident(cw.shape), _resident(cb.shape), _resident(g.shape), _resident(b.shape)],
        out_specs=[pl.BlockSpec((bb, CONF_W), lambda i: (i, 0)), pl.BlockSpec((bb, sw), lambda i: (i, 0))],
        out_shape=[jax.ShapeDtypeStruct((n, CONF_W), BF16), jax.ShapeDtypeStruct((n, sw), F32)],
        compiler_params=_cparams(1),
        name="sample_conf",
    )(h, st, cw, cb, g, b)


def _sample_ssd_prep_kernel(xbc_ref, dt_ref, st_ref, sw_ref, sb_ref, dtb_ref, alog_ref, rexp_ref,
                            nst_ref, xs_ref, bc_ref, xdt_t_ref, decay_ref):
    xr = xbc_ref[...]
    nk = SSD_CONV - 1
    acc = sb_ref[...] + sw_ref[nk:nk + 1, :] * xr
    for k in range(nk):
        acc = acc + sw_ref[k:k + 1, :] * st_ref[:, k * XBC_W:(k + 1) * XBC_W]
    xbc = _silu(acc)
    nst_ref[:, 0:(nk - 1) * XBC_W] = st_ref[:, XBC_W:nk * XBC_W]
    nst_ref[:, (nk - 1) * XBC_W:nk * XBC_W] = xr
    xs = xbc[:, :SSD_W]
    xs_ref[...] = xs
    bc_ref[...] = xbc[:, SSD_W:]
    dt = _softplus(dt_ref[...] + dtb_ref[...])
    rexp = rexp_ref[...]
    a_neg = -jnp.exp(alog_ref[...])
    xdt_t_ref[...] = (xs * _dot_f32(dt, rexp)).T
    decay_ref[...] = jnp.exp(dt * a_neg)


def _sample_ssd_prep(h, st, layer, sw, sb, dtb, alog, rexp):
    n = h.shape[0]
    stw = (SSD_CONV - 1) * XBC_W
    full = lambda r, c: pl.BlockSpec((r, c), lambda i: (0, 0))
    return pl.pallas_call(
        _sample_ssd_prep_kernel,
        grid=(1,),
        in_specs=[pl.BlockSpec((n, XBC_W), lambda i: (0, C_XBC // XBC_W)),
                  pl.BlockSpec((n, LANES), lambda i: (0, C_DT // LANES)),
                  pl.BlockSpec((None, n, stw), lambda i: (layer, 0, 0)),
                  full(*sw.shape), full(*sb.shape), full(*dtb.shape), full(*alog.shape), full(*rexp.shape)],
        out_specs=[full(n, stw), full(n, SSD_W), full(n, XBC_W - SSD_W), full(SSD_W, n), full(n, LANES)],
        out_shape=[jax.ShapeDtypeStruct((n, stw), F32), jax.ShapeDtypeStruct((n, SSD_W), F32),
                   jax.ShapeDtypeStruct((n, XBC_W - SSD_W), F32), jax.ShapeDtypeStruct((SSD_W, n), F32),
                   jax.ShapeDtypeStruct((n, LANES), F32)],
        compiler_params=_cparams(1),
        name="sample_ssd_prep",
    )(h, h, st, sw, sb, dtb, alog, rexp)


def _split_bf16(x):
    hi = x.astype(BF16)
    return hi, (x - hi.astype(F32)).astype(BF16)


def _sample_ssd_state_kernel(n, bg, decay_ref, st_ref, xdt_t_ref, b_ref, c_ref, *refs):
    nst_ref, y_t_ref = refs[-2:]
    h = pl.program_id(0)
    lane = lax.broadcasted_iota(jnp.int32, (SSD_HEAD_DIM, n), 1)
    x_hi, x_lo = _split_bf16(xdt_t_ref[...])
    b_hi, b_lo = _split_bf16(b_ref[...])
    b_cat = jnp.concatenate([b_hi, b_lo, b_hi], axis=0)
    c_bf = c_ref[...].astype(BF16)
    zero = jnp.zeros((SSD_HEAD_DIM, n), BF16)
    y_t = jnp.zeros((SSD_HEAD_DIM, n), F32)
    for b0 in range(0, n, bg):
        hi_m = jnp.concatenate([jnp.where(lane == b0 + j, x_hi, zero) for j in range(bg)], axis=0)
        lo_m = jnp.concatenate([jnp.where(lane == b0 + j, x_lo, zero) for j in range(bg)], axis=0)
        upd = _dot(jnp.concatenate([hi_m, hi_m, lo_m], axis=1), b_cat)
        hn = []
        for j in range(bg):
            hj = st_ref[b0 + j] * decay_ref[b0 + j, h] + upd[j * SSD_HEAD_DIM:(j + 1) * SSD_HEAD_DIM, :]
            nst_ref[b0 + j] = hj
            hn.append(hj.astype(BF16))
        yy = lax.dot_general(jnp.concatenate(hn, axis=0), c_bf, (((1,), (1,)), ((), ())),
                             preferred_element_type=F32)
        for j in range(bg):
            y_t = jnp.where(lane == b0 + j, yy[j * SSD_HEAD_DIM:(j + 1) * SSD_HEAD_DIM, :], y_t)
    y_t_ref[...] = y_t


def _sample_ssd_state(st, layer, decay, xdt_t, bc, stacked, bg=8):
    n = st.shape[1]
    hpg = SSD_HEADS // SSD_GROUPS
    col = pl.BlockSpec((SSD_HEAD_DIM, n), lambda h: (h, 0))
    state_block = pl.BlockSpec((None, n, None, SSD_HEAD_DIM, SSD_STATE), lambda h: (layer, 0, h, 0, 0))
    in_specs = [pl.BlockSpec(memory_space=pltpu.SMEM), state_block, col,
                pl.BlockSpec((n, SSD_STATE), lambda h: (0, h // hpg)),
                pl.BlockSpec((n, SSD_STATE), lambda h: (0, SSD_GROUPS + h // hpg))]
    args = [decay, st, xdt_t, bc, bc]
    aliases = {}
    if stacked is not None:
        in_specs.append(pl.BlockSpec(memory_space=pl.ANY))
        args.append(stacked)
        aliases = {len(args) - 1: 0}
    return pl.pallas_call(
        functools.partial(_sample_ssd_state_kernel, n, bg),
        grid=(SSD_HEADS,),
        in_specs=in_specs,
        out_specs=[state_block, col],
        out_shape=[jax.ShapeDtypeStruct(st.shape, F32), jax.ShapeDtypeStruct((SSD_W, n), F32)],
        input_output_aliases=aliases,
        compiler_params=_cparams(1),
        name="sample_ssd_state",
    )(*args)


def _sample_mix_out_kernel(y_t_ref, xs_ref, z_ref, a_ref, x_ref, wo_ref, dx_ref, ng_ref, g_ref, b_ref, o_ref):
    y = y_t_ref[...].T + dx_ref[...] * xs_ref[...]
    y = y * _silu(z_ref[...])
    ms = jnp.mean(y * y, -1, keepdims=True)
    y = y * lax.rsqrt(ms + LN_EPS) * ng_ref[...]
    acc = ALPHA * x_ref[...] + _dot(a_ref[...], wo_ref[0:CONF_W, :]) + _dot(y.astype(BF16), wo_ref[CONF_W:, :])
    o_ref[...] = _layer_norm(acc, g_ref[...], b_ref[...])


def _sample_mix_out(y_t, xs, h, a, x, wo, layer, dx, ng, g, b):
    n = x.shape[0]
    full = lambda arr: pl.BlockSpec(arr.shape, lambda i: (0,) * arr.ndim)
    return pl.pallas_call(
        _sample_mix_out_kernel,
        grid=(1,),
        in_specs=[full(y_t), full(xs), pl.BlockSpec((n, SSD_W), lambda i: (0, C_Z // SSD_W)), full(a),
                  full(x), _layer_resident(wo, layer), full(dx), full(ng), full(g), full(b)],
        out_specs=full(x),
        out_shape=jax.ShapeDtypeStruct(x.shape, F32),
        compiler_params=_cparams(1),
        name="sample_mix_out",
    )(y_t, xs, h, a, x, wo, dx, ng, g, b)


def _sample_attn_kernel(bb, q_ref, k_ref, v_ref, o_ref):
    nhalf = XA_HEAD_DIM // LANES
    rows_per_m = nhalf * XA_HEADS
    for j in range(bb):
        outs = []
        for hd in range(XA_HEADS):
            prod = None
            for half in range(nhalf):
                kk = k_ref[j, pl.ds(half * XA_HEADS + hd, MEM_LEN, stride=rows_per_m), :]
                lo = hd * XA_HEAD_DIM + half * LANES
                part = kk * q_ref[j, :, lo:lo + LANES]
                prod = part if prod is None else prod + part
            s = jnp.sum(prod, axis=-1, keepdims=True) * (XA_HEAD_DIM ** -0.5)
            e = jnp.exp(s - jnp.max(s, axis=0, keepdims=True))
            p = e / jnp.sum(e, axis=0, keepdims=True)
            for half in range(nhalf):
                vv = v_ref[j, pl.ds(half * XA_HEADS + hd, MEM_LEN, stride=rows_per_m), :]
                outs.append(jnp.sum(p * vv, axis=0, keepdims=True))
        o_ref[j] = jnp.concatenate(outs, axis=-1)


def _sample_attn(q, mk, mv, layer, bb=4):
    n, d = q.shape
    row = pl.BlockSpec((bb, 1, d), lambda i: (i, 0, 0))
    mem = pl.BlockSpec((None, bb, mk.shape[2], LANES), lambda i: (layer, i, 0, 0))
    o = pl.pallas_call(
        functools.partial(_sample_attn_kernel, bb),
        grid=(n // bb,),
        in_specs=[row, mem, mem],
        out_specs=row,
        out_shape=jax.ShapeDtypeStruct((n, 1, d), F32),
        compiler_params=_cparams(1),
        name="sample_attn",
    )(q.reshape(n, 1, d), mk, mv)
    return o.reshape(n, d)


FFN_LT = 2 * D_FF // LANES
FFN_ROWS = (FFN_CONV - 1) * FFN_LT


def _flat_ffn_state(s):
    nl, n = s.shape[0], s.shape[1]
    s = s.reshape(nl, n, FFN_CONV - 1, FFN_LT, LANES).transpose(0, 1, 3, 2, 4)
    return s.reshape(nl, n * FFN_ROWS, LANES)


def _unflat_ffn_state(s):
    nl, n = s.shape[0], s.shape[1] // FFN_ROWS
    s = s.reshape(nl, n, FFN_LT, FFN_CONV - 1, LANES).transpose(0, 1, 3, 2, 4)
    return s.reshape(nl, n, FFN_CONV - 1, 2 * D_FF)


def _sample_ffn_kernel(n, u_ref, st_ref, cw_ref, cb_ref, x_ref, wdn_ref, g_ref, b_ref, *refs):
    o_ref, nst_ref = refs[-2:]
    nk = FFN_CONV - 1
    conv = []
    for lt in range(FFN_LT):
        ls = slice(lt * LANES, (lt + 1) * LANES)
        u = u_ref[:, ls]
        hist = [st_ref[pl.ds(nk * lt + k, n, stride=FFN_ROWS), :] for k in range(nk)]
        cv = cb_ref[:, ls] + cw_ref[nk:nk + 1, ls] * u
        for k in range(nk):
            cv = cv + cw_ref[k:k + 1, ls] * hist[k]
        conv.append(cv)
        for k in range(nk - 1):
            nst_ref[pl.ds(nk * lt + k, n, stride=FFN_ROWS), :] = hist[k + 1]
        nst_ref[pl.ds(nk * lt + nk - 1, n, stride=FFN_ROWS), :] = u
    half = FFN_LT // 2
    f = jnp.concatenate([_silu(conv[half + j]) * conv[j] for j in range(half)], axis=-1)
    acc = ALPHA * x_ref[...] + _dot(f.astype(BF16), wdn_ref[...])
    o_ref[...] = _layer_norm(acc, g_ref[...], b_ref[...])


def _sample_ffn(u, st, layer, cw, cb, x, wdn, g, b, stacked):
    n = x.shape[0]
    one = lambda arr: pl.BlockSpec(arr.shape, lambda i: (0,) * arr.ndim, pipeline_mode=pl.Buffered(1))
    state_block = pl.BlockSpec((None, n * FFN_ROWS, LANES), lambda i: (layer, 0, 0), pipeline_mode=pl.Buffered(1))
    in_specs = [one(u), state_block, one(cw), one(cb), one(x), _layer_resident(wdn, layer), one(g), one(b)]
    args = [u, st, cw, cb, x, wdn, g, b]
    aliases = {}
    if stacked is not None:
        in_specs.append(pl.BlockSpec(memory_space=pl.ANY))
        args.append(stacked)
        aliases = {len(args) - 1: 1}
    return pl.pallas_call(
        functools.partial(_sample_ffn_kernel, n),
        grid=(1,),
        in_specs=in_specs,
        out_specs=[pl.BlockSpec(x.shape, lambda i: (0, 0)),
                   pl.BlockSpec((None, n * FFN_ROWS, LANES), lambda i: (layer, 0, 0))],
        out_shape=[jax.ShapeDtypeStruct(x.shape, F32), jax.ShapeDtypeStruct(st.shape, F32)],
        input_output_aliases=aliases,
        compiler_params=_cparams(1),
        name="sample_ffn",
    )(*args)


def kernel(x_prompt, x_sample, cache_mem_k, cache_mem_v, state_conf_conv, state_ssd_conv, state_ssd,
           state_ffn_conv, mem_prompt, w_in, conf_conv_w, conf_conv_b, conf_ln_g, conf_ln_b, ssd_conv_w,
           ssd_conv_b, ssd_dt_bias, ssd_a_log, ssd_d, ssd_norm_g, w_out, ln_mix_g, ln_mix_b, xa_wq, xa_wk,
           xa_wv, xa_wo, ln_xa_g, ln_xa_b, ffn_w_up, ffn_conv_w, ffn_conv_b, ffn_w_down, ln_ffn_g, ln_ffn_b):
    bp, t, d = x_prompt.shape
    ns = x_sample.shape[0]
    nl = DEPTH
    tt_mix, tt_xa, tt_ffn = min(MIXER_TILE, t), min(XATTN_TILE, t), min(FFN_TILE, t)

    row = lambda p, l: p[l][None, :]
    pad_heads = lambda p, l: jnp.pad(p[l], (0, LANES - SSD_HEADS))[None, :]
    rexp = jnp.pad(jnp.repeat(jnp.eye(SSD_HEADS, dtype=F32), SSD_HEAD_DIM, axis=1),
                   ((0, LANES - SSD_HEADS), (0, 0)))

    glu_w = w_in[:, :, :C_Z].reshape(nl, d, 2, CONF_W // PROJ_COLS, PROJ_COLS)
    glu_w = glu_w.transpose(0, 1, 3, 2, 4).reshape(nl, d, C_Z)
    win = jnp.concatenate([glu_w, w_in[:, :, C_Z:]], axis=-1)
    win = jnp.pad(win, ((0, 0), (0, 0), (0, D_IN_PAD - w_in.shape[2]))).astype(BF16)
    wout, wq, wo = w_out.astype(BF16), xa_wq.astype(BF16), xa_wo.astype(BF16)
    wup, wdn = ffn_w_up.astype(BF16), ffn_w_down.astype(BF16)

    mem2d = mem_prompt.reshape(bp * MEM_LEN, d)
    mk_p = _mem_proj(mem2d, xa_wk.astype(BF16), 512).reshape(nl, bp, MEM_ROWS, LANES)
    mv_p = _mem_proj(mem2d, xa_wv.astype(BF16), 512).reshape(nl, bp, MEM_ROWS, LANES)
    mk_s, mv_s = _flat_cache(cache_mem_k), _flat_cache(cache_mem_v)
    conf_s = state_conf_conv.reshape(nl, ns, (CONF_KERNEL - 1) * CONF_W)
    ssdc_s = state_ssd_conv.reshape(nl, ns, (SSD_CONV - 1) * XBC_W)
    ffn_s = _flat_ffn_state(state_ffn_conv)

    xp = _permute_rows(x_prompt, inverse=False)
    xs_ = x_sample.reshape(ns, d)
    st_p = ([], [], [], [])
    st_s = ([], [])
    nst = nfs = None
    for l in range(nl):
        cw, cb = conf_conv_w[l], row(conf_conv_b, l)
        cg, cbeta = row(conf_ln_g, l), row(conf_ln_b, l)
        sw, sb = ssd_conv_w[l], row(ssd_conv_b, l)
        dtb, alog = pad_heads(ssd_dt_bias, l), pad_heads(ssd_a_log, l)
        dx = jnp.repeat(ssd_d[l], SSD_HEAD_DIM)[None, :]
        ng = row(ssd_norm_g, l)
        g1, b1 = row(ln_mix_g, l), row(ln_mix_b, l)
        g2, b2 = row(ln_xa_g, l), row(ln_xa_b, l)
        g3, b3 = row(ln_ffn_g, l), row(ln_ffn_b, l)
        fcw, fcb = ffn_conv_w[l], row(ffn_conv_b, l)

        x1, cst, sst, hst = _prompt_mixer(xp, l, win, wout, cw, cb, cg, cbeta, sw, sb, dtb, alog, dx, ng, g1, b1,
                                          tt_mix)
        x2 = _prompt_xattn(x1, mk_p, mv_p, l, wq, wo, g2, b2, tt_xa)
        xp, fst = _prompt_ffn(x2, l, wup, wdn, fcw, fcb, g3, b3, tt_ffn)
        for lst, s in zip(st_p, (cst, sst, hst, fst)):
            lst.append(s)

        h = _mm_k(xs_, win, l)
        a_s, ncs = _sample_conf(h, conf_s, l, cw, cb, cg, cbeta)
        nss, xs1, bc, xdt_t, decay = _sample_ssd_prep(h, ssdc_s, l, sw, sb, dtb, alog, rexp)
        nst, y_t = _sample_ssd_state(state_ssd, l, decay[:, :SSD_HEADS], xdt_t, bc, nst)
        x1s = _sample_mix_out(y_t, xs1, h, a_s, xs_, wout, l, dx, ng, g1, b1)
        qs = _mm_k(x1s, wq, l)
        os_ = _sample_attn(qs, mk_s, mv_s, l)
        x2s = _proj_ln([os_], [wo], l, x1s, g2, b2, ns)
        us = _mm_k(x2s, wup, l)
        xs_, nfs = _sample_ffn(us, ffn_s, l, fcw, fcb, x2s, wdn, g3, b3, nfs)
        for lst, s in zip(st_s, (ncs, nss)):
            lst.append(s)

    st_p = [jnp.stack(s) for s in st_p]
    st_s = [jnp.stack(s) for s in st_s]
    return (_permute_rows(xp, inverse=True), xs_.reshape(ns, 1, d), st_p[0], st_p[1], st_p[2], st_p[3],
            _unflat_cache(mk_p), _unflat_cache(mv_p),
            st_s[0].reshape(nl, ns, CONF_KERNEL - 1, CONF_W), st_s[1].reshape(nl, ns, SSD_CONV - 1, XBC_W),
            nst, _unflat_ffn_state(nfs))
```

```python
import functools

import jax
import jax.numpy as jnp
from jax import lax
from jax.experimental import pallas as pl
from jax.experimental.pallas import tpu as pltpu

F32 = jnp.float32
BF16 = jnp.bfloat16
HIGHEST = lax.Precision.HIGHEST

D_MODEL = 1024
DEPTH = 4
CONF_W = 1024
CONF_KERNEL = 31
SSD_HEAD_DIM = 64
SSD_HEADS = 16
SSD_W = 1024
SSD_GROUPS = 2
SSD_STATE = 128
SSD_CONV = 4
SSD_CHUNK = 128
XBC_W = SSD_W + 2 * SSD_GROUPS * SSD_STATE
MEM_LEN = 256
XA_HEADS = 4
XA_HEAD_DIM = 256
D_FF = 2816
FFN_CONV = 3
ALPHA = (2.0 * DEPTH) ** 0.25
LN_EPS = 1e-5

LANES = 128
C_GLU, C_Z, C_XBC, C_DT = 0, 2 * CONF_W, 2 * CONF_W + SSD_W, 2 * CONF_W + SSD_W + XBC_W
D_IN_PAD = C_DT + LANES
PROJ_COLS = 256
MIXER_TILE = 256
XATTN_TILE = 1024
FFN_TILE = 512
FFN_COLS = 256
VMEM_LIMIT = 56 * 1024 * 1024


def _cparams(n_grid_dims):
    return pltpu.CompilerParams(dimension_semantics=("arbitrary",) * n_grid_dims,
                                vmem_limit_bytes=VMEM_LIMIT)


def _resident(shape):
    nd = len(shape)
    return pl.BlockSpec(shape, lambda *_: (0,) * nd, pipeline_mode=pl.Buffered(1))


def _layer_resident(w, layer):
    return pl.BlockSpec((None,) + w.shape[1:], lambda *_: (layer, 0, 0), pipeline_mode=pl.Buffered(1))


def _sigmoid(x):
    return jax.nn.sigmoid(x)


def _silu(x):
    return x * _sigmoid(x)


def _softplus(x):
    return jnp.maximum(x, 0.0) + jnp.log(1.0 + jnp.exp(-jnp.abs(x)))


def _layer_norm(x, g, b):
    mu = jnp.mean(x, -1, keepdims=True)
    xc = x - mu
    var = jnp.mean(xc * xc, -1, keepdims=True)
    return xc * lax.rsqrt(var + LN_EPS) * g + b


def _dot(a, b):
    return jnp.dot(a, b, preferred_element_type=F32)


def _dot_f32(a, b):
    return jnp.dot(a, b, preferred_element_type=F32, precision=HIGHEST)


CHUNK = SSD_CHUNK
SLABS = CHUNK // 8


def _permuted_row(token):
    return 8 * (token % SLABS) + token // SLABS


def _wrapped_slab(cur, prev, shift, sub):
    return jnp.where(sub >= shift, pltpu.roll(cur, shift, 0), pltpu.roll(prev, shift, 0))


def _fill_history(src, base, eseq, nk, width):
    sub = lax.broadcasted_iota(jnp.int32, (8, width), 0)
    for i in range(SLABS):
        for shift in (1, 2):
            p = i - SLABS * shift + (nk - 1)
            if 0 <= p < nk - 1:
                cur = src[base + 8 * i:base + 8 * (i + 1), :]
                prev = src[base - CHUNK + 8 * i:base - CHUNK + 8 * (i + 1), :]
                eseq[8 * p:8 * (p + 1), :] = _wrapped_slab(cur, prev, shift, sub)
    eseq[8 * (nk - 1):8 * (nk - 1) + CHUNK, :] = src[base:base + CHUNK, :]


def _conv_from_history(eseq, w_ref, nk, lo, rows):
    r0, nr = rows
    acc = w_ref[0:1, lo:lo + LANES] * eseq[r0:r0 + nr, lo:lo + LANES]
    for k in range(1, nk):
        acc = acc + w_ref[k:k + 1, lo:lo + LANES] * eseq[8 * k + r0:8 * k + r0 + nr, lo:lo + LANES]
    return acc


def _permute_rows_kernel(tt, inverse, *refs):
    in_refs, o_ref = refs[:-1], refs[-1]
    for c in range(tt // CHUNK):
        for lt, x_ref in enumerate(in_refs):
            ls = slice(lt * LANES, (lt + 1) * LANES)
            if not inverse:
                for i in range(SLABS):
                    o_ref[0, c * CHUNK + 8 * i:c * CHUNK + 8 * (i + 1), ls] = \
                        x_ref[0, pl.ds(c * CHUNK + i, 8, stride=SLABS), :]
            else:
                for s in range(8):
                    for i0 in range(0, SLABS, 8):
                        tok = c * CHUNK + s * SLABS + i0
                        o_ref[0, tok:tok + 8, ls] = x_ref[0, pl.ds(c * CHUNK + 8 * i0 + s, 8, stride=8), :]


def _permute_rows(x, inverse, tt=512):
    bsz, t, w = x.shape
    tt = min(tt, t)
    lane_tile = lambda lt: pl.BlockSpec((1, tt, LANES), lambda bi, i: (bi, i, lt))
    return pl.pallas_call(
        functools.partial(_permute_rows_kernel, tt, inverse),
        grid=(bsz, t // tt),
        in_specs=[lane_tile(lt) for lt in range(w // LANES)],
        out_specs=pl.BlockSpec((1, tt, w), lambda bi, i: (bi, i, 0)),
        out_shape=jax.ShapeDtypeStruct(x.shape, x.dtype),
        compiler_params=_cparams(2),
        name="permute_rows",
    )(*([x] * (w // LANES)))


def _permuted_time(n):
    r = lax.broadcasted_iota(jnp.int32, (CHUNK, CHUNK), n)
    return (r & 7) * SLABS + (r >> 3)


def _prompt_mixer_kernel(tt, x_ref, win_ref, wout_ref, cw_ref, cb_ref, cg_ref, cbeta_ref, sw_ref, sb_ref,
                         dtb_ref, alog_ref, dx_ref, ng_ref, g_ref, b_ref,
                         o_ref, cst_ref, sst_ref, hst_ref,
                         abuf, xbuf, zbuf, dtbuf, eseq, eseq2, cvbuf, xact, ygbuf, abf, ybf, ht):
    t = pl.program_id(1)
    nch = tt // CHUNK

    @pl.when(t == 0)
    def _():
        abuf[0:CHUNK, :] = jnp.zeros((CHUNK, CONF_W), F32)
        xbuf[0:CHUNK, :] = jnp.zeros((CHUNK, XBC_W), F32)
        ht[...] = jnp.zeros(ht.shape, F32)

    xb = x_ref[0].astype(BF16)
    nb = PROJ_COLS
    for j in range(CONF_W // nb):
        val = _dot(xb, win_ref[:, C_GLU + nb * j:C_GLU + nb * (j + 1)])
        gate = _dot(xb, win_ref[:, C_GLU + CONF_W + nb * j:C_GLU + CONF_W + nb * (j + 1)])
        abuf[CHUNK:CHUNK + tt, j * nb:(j + 1) * nb] = val * _sigmoid(gate)
    for j in range(SSD_W // nb):
        zbuf[:, j * nb:(j + 1) * nb] = _dot(xb, win_ref[:, C_Z + j * nb:C_Z + (j + 1) * nb])
    for j in range(XBC_W // nb):
        xbuf[CHUNK:CHUNK + tt, j * nb:(j + 1) * nb] = _dot(xb, win_ref[:, C_XBC + j * nb:C_XBC + (j + 1) * nb])
    dtbuf[...] = _softplus(_dot(xb, win_ref[:, C_DT:C_DT + LANES]) + dtb_ref[...])

    tl, ts = _permuted_time(0), _permuted_time(1)
    causal = tl >= ts
    causal_f = causal.astype(F32)
    a_neg = -jnp.exp(alog_ref[...])
    first_head = lax.broadcasted_iota(jnp.int32, (CHUNK, LANES), 1) < SSD_HEAD_DIM
    rb = 32

    for c in range(nch):
        base = CHUNK * (c + 1)
        r0 = CHUNK * c
        _fill_history(abuf, base, eseq, CONF_KERNEL, CONF_W)
        for lt in range(CONF_W // LANES):
            for r in range(CHUNK // rb):
                cvbuf[r * rb:(r + 1) * rb, lt * LANES:(lt + 1) * LANES] = _conv_from_history(
                    eseq, cw_ref, CONF_KERNEL, lt * LANES, (r * rb, rb))
        y = _layer_norm(cvbuf[...] + cb_ref[...], cg_ref[...], cbeta_ref[...])
        abf[r0:r0 + CHUNK, :] = _silu(y).astype(BF16)

        _fill_history(xbuf, base, eseq2, SSD_CONV, XBC_W)
        for lt in range(XBC_W // LANES):
            lo = lt * LANES
            acc = _conv_from_history(eseq2, sw_ref, SSD_CONV, lo, (0, CHUNK))
            xact[:, lo:lo + LANES] = _silu(acc + sb_ref[:, lo:lo + LANES])

        dt = dtbuf[r0:r0 + CHUNK, :]
        acum = _dot_f32(causal_f, dt * a_neg)
        act = acum.T
        ss = jnp.zeros((CHUNK, 1), F32)
        for g in range(SSD_GROUPS):
            bm = xact[:, SSD_W + g * SSD_STATE:SSD_W + (g + 1) * SSD_STATE]
            cmb = xact[:, SSD_W + (SSD_GROUPS + g) * SSD_STATE:SSD_W + (SSD_GROUPS + g + 1) * SSD_STATE].astype(BF16)
            cb = lax.dot_general(cmb, bm.astype(BF16), (((1,), (1,)), ((), ())), preferred_element_type=F32)
            bmt = bm.T.astype(BF16)
            for jj in range(SSD_W // SSD_GROUPS // LANES):
                lo = (g * (SSD_W // SSD_GROUPS // LANES) + jj) * LANES
                h0 = lo // SSD_HEAD_DIM
                dtl = jnp.where(first_head, dt[:, h0:h0 + 1], dt[:, h0 + 1:h0 + 2])
                acl = jnp.where(first_head, acum[:, h0:h0 + 1], acum[:, h0 + 1:h0 + 2])
                xs = xact[:, lo:lo + LANES]
                dtx = xs * dtl
                dtxb = dtx.astype(BF16)
                a_last = acl[CHUNK - 1:CHUNK, :]
                hprev = ht[:, lo:lo + LANES]
                y_off = _dot(cmb, hprev.astype(BF16)) * jnp.exp(acl)
                yd = []
                for h in (h0, h0 + 1):
                    decay = jnp.exp(jnp.where(causal, acum[:, h:h + 1] - act[h:h + 1, :], -1e30))
                    yd.append(_dot((cb * decay).astype(BF16), dtxb))
                y = jnp.where(first_head, yd[0], yd[1]) + y_off + dx_ref[:, lo:lo + LANES] * xs
                yg = y * _silu(zbuf[r0:r0 + CHUNK, lo:lo + LANES])
                ygbuf[:, lo:lo + LANES] = yg
                ss = ss + jnp.sum(yg * yg, -1, keepdims=True)
                ht[:, lo:lo + LANES] = hprev * jnp.exp(a_last) + _dot(bmt, (dtx * jnp.exp(a_last - acl)).astype(BF16))
        inv = lax.rsqrt(ss * (1.0 / SSD_W) + LN_EPS)
        ybf[r0:r0 + CHUNK, :] = (ygbuf[...] * inv * ng_ref[...]).astype(BF16)

    mix = _dot(abf[...], wout_ref[0:CONF_W, :]) + _dot(ybf[...], wout_ref[CONF_W:, :])
    o_ref[0] = _layer_norm(ALPHA * x_ref[0] + mix, g_ref[...], b_ref[...])

    @pl.when(t == pl.num_programs(1) - 1)
    def _():
        for r in range(CONF_KERNEL - 1):
            src_row = tt + _permuted_row(CHUNK - (CONF_KERNEL - 1) + r)
            cst_ref[0, r:r + 1, :] = abuf[src_row:src_row + 1, :]
        for r in range(SSD_CONV - 1):
            src_row = tt + _permuted_row(CHUNK - (SSD_CONV - 1) + r)
            sst_ref[0, r:r + 1, :] = xbuf[src_row:src_row + 1, :]
        hst_ref[0] = ht[...].T.reshape(SSD_HEADS, SSD_HEAD_DIM, SSD_STATE)

    abuf[0:CHUNK, :] = abuf[tt:tt + CHUNK, :]
    xbuf[0:CHUNK, :] = xbuf[tt:tt + CHUNK, :]


def _prompt_mixer(x, layer, win, wout, cw, cb, cg, cbeta, sw, sb, dtb, alog, dx, ng, g, b, tt):
    bsz, t, d = x.shape
    tile = pl.BlockSpec((1, tt, d), lambda bi, i: (bi, i, 0))
    per_b = lambda *s: pl.BlockSpec((1,) + s, lambda bi, i: (bi,) + (0,) * len(s))
    consts = (cw, cb, cg, cbeta, sw, sb, dtb, alog, dx, ng, g, b)
    return pl.pallas_call(
        functools.partial(_prompt_mixer_kernel, tt),
        grid=(bsz, t // tt),
        in_specs=[tile, _layer_resident(win, layer), _layer_resident(wout, layer)]
        + [_resident(a.shape) for a in consts],
        out_specs=[tile, per_b(CONF_KERNEL - 1, CONF_W), per_b(SSD_CONV - 1, XBC_W),
                   per_b(SSD_HEADS, SSD_HEAD_DIM, SSD_STATE)],
        out_shape=[jax.ShapeDtypeStruct((bsz, t, d), F32),
                   jax.ShapeDtypeStruct((bsz, CONF_KERNEL - 1, CONF_W), F32),
                   jax.ShapeDtypeStruct((bsz, SSD_CONV - 1, XBC_W), F32),
                   jax.ShapeDtypeStruct((bsz, SSD_HEADS, SSD_HEAD_DIM, SSD_STATE), F32)],
        scratch_shapes=[pltpu.VMEM((CHUNK + tt, CONF_W), F32),
                        pltpu.VMEM((CHUNK + tt, XBC_W), F32),
                        pltpu.VMEM((tt, SSD_W), F32),
                        pltpu.VMEM((tt, LANES), F32),
                        pltpu.VMEM((8 * (CONF_KERNEL - 1) + CHUNK, CONF_W), F32),
                        pltpu.VMEM((8 * (SSD_CONV - 1) + CHUNK, XBC_W), F32),
                        pltpu.VMEM((CHUNK, CONF_W), F32),
                        pltpu.VMEM((CHUNK, XBC_W), F32),
                        pltpu.VMEM((CHUNK, SSD_W), F32),
                        pltpu.VMEM((tt, CONF_W), BF16),
                        pltpu.VMEM((tt, SSD_W), BF16),
                        pltpu.VMEM((SSD_STATE, SSD_W), F32)],
        compiler_params=_cparams(2),
        name="prompt_mixer",
    )(x, win, wout, *consts)


def _proj_ln_kernel(n_in, *refs):
    in_refs, w_refs = refs[:n_in], refs[n_in:2 * n_in]
    x_ref, g_ref, b_ref, o_ref = refs[2 * n_in:]
    acc = ALPHA * x_ref[...]
    for a_ref, w_ref in zip(in_refs, w_refs):
        acc = acc + _dot(a_ref[...].astype(BF16), w_ref[...])
    o_ref[...] = _layer_norm(acc, g_ref[...], b_ref[...])


def _proj_ln(ins, ws, layer, resid, g, b, tm):
    m, d = resid.shape
    n_in = len(ins)
    rows = lambda a: pl.BlockSpec((tm, a.shape[1]), lambda i: (i, 0))
    return pl.pallas_call(
        functools.partial(_proj_ln_kernel, n_in),
        grid=(m // tm,),
        in_specs=[rows(a) for a in ins] + [_layer_resident(w, layer) for w in ws]
        + [rows(resid), _resident(g.shape), _resident(b.shape)],
        out_specs=rows(resid),
        out_shape=jax.ShapeDtypeStruct((m, d), F32),
        compiler_params=_cparams(1),
        name="proj_ln",
    )(*ins, *ws, resid, g, b)


def _prompt_xattn_kernel(x_ref, k_ref, v_ref, wq_ref, wo_ref, g_ref, b_ref, o_ref):
    x = x_ref[0]
    q = _dot(x.astype(BF16), wq_ref[...])
    acc = ALPHA * x
    for hd in range(XA_HEADS):
        sl = slice(hd * XA_HEAD_DIM, (hd + 1) * XA_HEAD_DIM)
        qh = q[:, sl].astype(BF16)
        kh = _load_flat_head(k_ref, 0, hd).astype(BF16)
        s = lax.dot_general(qh, kh, (((1,), (1,)), ((), ())), preferred_element_type=F32)
        s = s * (XA_HEAD_DIM ** -0.5)
        e = jnp.exp(s - jnp.max(s, -1, keepdims=True))
        p = e / jnp.sum(e, -1, keepdims=True)
        oh = _dot(p.astype(BF16), _load_flat_head(v_ref, 0, hd).astype(BF16))
        acc = acc + _dot(oh.astype(BF16), wo_ref[sl, :])
    o_ref[0] = _layer_norm(acc, g_ref[...], b_ref[...])


def _prompt_xattn(x, mk, mv, layer, wq, wo, g, b, tt):
    bsz, t, d = x.shape
    tile = pl.BlockSpec((1, tt, d), lambda bi, i: (bi, i, 0))
    mem = pl.BlockSpec((None, 1, MEM_ROWS, LANES), lambda bi, i: (layer, bi, 0, 0))
    return pl.pallas_call(
        _prompt_xattn_kernel,
        grid=(bsz, t // tt),
        in_specs=[tile, mem, mem, _layer_resident(wq, layer), _layer_resident(wo, layer), _resident(g.shape),
                  _resident(b.shape)],
        out_specs=tile,
        out_shape=jax.ShapeDtypeStruct((bsz, t, d), F32),
        compiler_params=_cparams(2),
        name="prompt_xattn",
    )(x, mk, mv, wq, wo, g, b)


def _prompt_ffn_kernel(tt, x_ref, wup_ref, wdn_ref, cw_ref, cb_ref, g_ref, b_ref, o_ref, st_ref, hst):
    t = pl.program_id(1)
    nch = tt // CHUNK

    @pl.when(t == 0)
    def _():
        hst[...] = jnp.zeros(hst.shape, F32)

    xb = x_ref[0].astype(BF16)
    sub = lax.broadcasted_iota(jnp.int32, (8, FFN_COLS), 0)
    acc = jnp.zeros((tt, D_MODEL), F32)
    for j in range(D_FF // FFN_COLS):
        conv = []
        for half in range(2):
            c0 = half * D_FF + j * FFN_COLS
            cs = slice(c0, c0 + FFN_COLS)
            u = _dot(xb, wup_ref[:, cs])
            w0, w1, w2 = cw_ref[0:1, cs], cw_ref[1:2, cs], cw_ref[2:3, cs]
            outs = []
            for c in range(nch):
                cur = u[c * CHUNK:(c + 1) * CHUNK, :]
                prev = hst[:, cs] if c == 0 else u[c * CHUNK - 16:c * CHUNK, :]
                e2 = _wrapped_slab(cur[CHUNK - 16:CHUNK - 8, :], prev[0:8, :], 1, sub)
                e1 = _wrapped_slab(cur[CHUNK - 8:CHUNK, :], prev[8:16, :], 1, sub)
                back1 = jnp.concatenate([e1, cur[0:CHUNK - 8, :]], axis=0)
                back2 = jnp.concatenate([e2, e1, cur[0:CHUNK - 16, :]], axis=0)
                outs.append(cb_ref[:, cs] + w2 * cur + w1 * back1 + w0 * back2)
            hst[:, cs] = u[tt - 16:tt, :]
            conv.append(jnp.concatenate(outs, axis=0))
        f = _silu(conv[1]) * conv[0]
        acc = acc + _dot(f.astype(BF16), wdn_ref[j * FFN_COLS:(j + 1) * FFN_COLS, :])
    o_ref[0] = _layer_norm(ALPHA * x_ref[0] + acc, g_ref[...], b_ref[...])
    st_ref[0, 0:1, :] = hst[7:8, :]
    st_ref[0, 1:2, :] = hst[15:16, :]


def _prompt_ffn(x, layer, wup, wdn, cw, cb, g, b, tt):
    bsz, t, d = x.shape
    tile = pl.BlockSpec((1, tt, d), lambda bi, i: (bi, i, 0))
    return pl.pallas_call(
        functools.partial(_prompt_ffn_kernel, tt),
        grid=(bsz, t // tt),
        in_specs=[tile, _layer_resident(wup, layer), _layer_resident(wdn, layer), _resident(cw.shape),
                  _resident(cb.shape), _resident(g.shape), _resident(b.shape)],
        out_specs=[tile, pl.BlockSpec((1, FFN_CONV - 1, 2 * D_FF), lambda bi, i: (bi, 0, 0))],
        out_shape=[jax.ShapeDtypeStruct((bsz, t, d), F32),
                   jax.ShapeDtypeStruct((bsz, FFN_CONV - 1, 2 * D_FF), F32)],
        scratch_shapes=[pltpu.VMEM((16, 2 * D_FF), F32)],
        compiler_params=_cparams(2),
        name="prompt_ffn",
    )(x, wup, wdn, cw, cb, g, b)


MEM_ROWS = MEM_LEN * (XA_HEAD_DIM // LANES) * XA_HEADS


def _flat_cache(c):
    nl, n = c.shape[0], c.shape[1]
    c = c.reshape(nl, n, MEM_LEN, XA_HEADS, XA_HEAD_DIM // LANES, LANES)
    return c.transpose(0, 1, 2, 4, 3, 5).reshape(nl, n, MEM_ROWS, LANES)


def _unflat_cache(c):
    nl, n = c.shape[0], c.shape[1]
    c = c.reshape(nl, n, MEM_LEN, XA_HEAD_DIM // LANES, XA_HEADS, LANES)
    return c.transpose(0, 1, 2, 4, 3, 5).reshape(nl, n, MEM_LEN, XA_HEADS, XA_HEAD_DIM)


def _load_flat_head(ref, idx, hd):
    rows_per_m = MEM_ROWS // MEM_LEN
    return jnp.concatenate([ref[idx, pl.ds(half * XA_HEADS + hd, MEM_LEN, stride=rows_per_m), :]
                            for half in range(XA_HEAD_DIM // LANES)], axis=-1)


def _mem_proj_kernel(tm, x_ref, w_ref, o_ref):
    o = _dot(x_ref[...].astype(BF16), w_ref[...])
    rows_per_m = MEM_ROWS // MEM_LEN
    for hd in range(XA_HEADS):
        for half in range(XA_HEAD_DIM // LANES):
            lo = hd * XA_HEAD_DIM + half * LANES
            o_ref[pl.ds(half * XA_HEADS + hd, tm, stride=rows_per_m), :] = o[:, lo:lo + LANES]


def _mem_proj(mem, w, tm):
    m, d = mem.shape
    nl = w.shape[0]
    rows_per_m = MEM_ROWS // MEM_LEN
    return pl.pallas_call(
        functools.partial(_mem_proj_kernel, tm),
        grid=(nl, m // tm),
        in_specs=[pl.BlockSpec((tm, d), lambda l, i: (i, 0)),
                  pl.BlockSpec((None,) + w.shape[1:], lambda l, i: (l, 0, 0))],
        out_specs=pl.BlockSpec((None, tm * rows_per_m, LANES), lambda l, i: (l, i, 0)),
        out_shape=jax.ShapeDtypeStruct((nl, m * rows_per_m, LANES), F32),
        compiler_params=_cparams(2),
        name="mem_proj",
    )(mem, w)


def _mm_k_kernel(x_ref, w_ref, o_ref):
    @pl.when(pl.program_id(0) == 0)
    def _():
        o_ref[...] = jnp.zeros(o_ref.shape, F32)

    o_ref[...] += _dot(x_ref[...].astype(BF16), w_ref[...])


def _mm_k(x, w, layer, tk=256):
    m, k = x.shape
    n = w.shape[2]
    return pl.pallas_call(
        _mm_k_kernel,
        grid=(k // tk,),
        in_specs=[pl.BlockSpec((m, tk), lambda i: (0, i)), pl.BlockSpec((None, tk, n), lambda i: (layer, i, 0))],
        out_specs=pl.BlockSpec((m, n), lambda i: (0, 0)),
        out_shape=jax.ShapeDtypeStruct((m, n), F32),
        compiler_params=_cparams(1),
        name="mm_k",
    )(x, w)


def _sample_conf_kernel(h_ref, st_ref, cw_ref, cb_ref, g_ref, b_ref, *refs):
    a_ref, nst_ref = refs[-2:]
    a = h_ref[:, :CONF_W] * _sigmoid(h_ref[:, CONF_W:])
    nk = CONF_KERNEL - 1
    acc = cb_ref[...] + cw_ref[nk:nk + 1, :] * a
    for k in range(nk):
        acc = acc + cw_ref[k:k + 1, :] * st_ref[k]
    a_ref[...] = _silu(_layer_norm(acc, g_ref[...], b_ref[...])).astype(BF16)
    for k in range(nk - 1):
        nst_ref[k] = st_ref[k + 1]
    nst_ref[nk - 1] = a


def _sample_conf(h, st, layer, cw, cb, g, b, stacked, bb=32):
    n = h.shape[0]
    nk = CONF_KERNEL - 1
    state_block = pl.BlockSpec((None, nk, bb, CONF_W), lambda i: (layer, 0, i, 0))
    in_specs = [pl.BlockSpec((bb, 2 * CONF_W), lambda i: (i, 0)), state_block,
                _resident(cw.shape), _resident(cb.shape), _resident(g.shape), _resident(b.shape)]
    args = [h, st, cw, cb, g, b]
    aliases = {}
    if stacked is not None:
        in_specs.append(pl.BlockSpec(memory_space=pl.ANY))
        args.append(stacked)
        aliases = {len(args) - 1: 1}
    return pl.pallas_call(
        _sample_conf_kernel,
        grid=(n // bb,),
        in_specs=in_specs,
        out_specs=[pl.BlockSpec((bb, CONF_W), lambda i: (i, 0)), state_block],
        out_shape=[jax.ShapeDtypeStruct((n, CONF_W), BF16), jax.ShapeDtypeStruct(st.shape, F32)],
        input_output_aliases=aliases,
        compiler_params=_cparams(1),
        name="sample_conf",
    )(*args)


def _sample_ssd_prep_kernel(xbc_ref, dt_ref, st_ref, sw_ref, sb_ref, dtb_ref, alog_ref, rexp_ref,
                            nst_ref, xs_ref, bc_ref, xdt_t_ref, decay_ref):
    xr = xbc_ref[...]
    nk = SSD_CONV - 1
    acc = sb_ref[...] + sw_ref[nk:nk + 1, :] * xr
    for k in range(nk):
        acc = acc + sw_ref[k:k + 1, :] * st_ref[k]
    xbc = _silu(acc)
    for k in range(nk - 1):
        nst_ref[k] = st_ref[k + 1]
    nst_ref[nk - 1] = xr
    xs = xbc[:, :SSD_W]
    xs_ref[...] = xs
    bc_ref[...] = xbc[:, SSD_W:]
    dt = _softplus(dt_ref[...] + dtb_ref[...])
    rexp = rexp_ref[...]
    a_neg = -jnp.exp(alog_ref[...])
    xdt_t_ref[...] = (xs * _dot_f32(dt, rexp)).T
    decay_ref[...] = jnp.exp(dt * a_neg)


def _sample_ssd_prep(h, st, layer, sw, sb, dtb, alog, rexp):
    n = h.shape[0]
    nk = SSD_CONV - 1
    full = lambda r, c: pl.BlockSpec((r, c), lambda i: (0, 0))
    return pl.pallas_call(
        _sample_ssd_prep_kernel,
        grid=(1,),
        in_specs=[pl.BlockSpec((n, XBC_W), lambda i: (0, C_XBC // XBC_W)),
                  pl.BlockSpec((n, LANES), lambda i: (0, C_DT // LANES)),
                  pl.BlockSpec((None, nk, n, XBC_W), lambda i: (layer, 0, 0, 0)),
                  full(*sw.shape), full(*sb.shape), full(*dtb.shape), full(*alog.shape), full(*rexp.shape)],
        out_specs=[pl.BlockSpec((nk, n, XBC_W), lambda i: (0, 0, 0)), full(n, SSD_W), full(n, XBC_W - SSD_W),
                   full(SSD_W, n), full(n, LANES)],
        out_shape=[jax.ShapeDtypeStruct((nk, n, XBC_W), F32), jax.ShapeDtypeStruct((n, SSD_W), F32),
                   jax.ShapeDtypeStruct((n, XBC_W - SSD_W), F32), jax.ShapeDtypeStruct((SSD_W, n), F32),
                   jax.ShapeDtypeStruct((n, LANES), F32)],
        compiler_params=_cparams(1),
        name="sample_ssd_prep",
    )(h, h, st, sw, sb, dtb, alog, rexp)


def _split_bf16(x):
    hi = x.astype(BF16)
    return hi, (x - hi.astype(F32)).astype(BF16)


def _sample_ssd_state_kernel(n, bg, decay_ref, st_ref, xdt_t_ref, b_ref, c_ref, *refs):
    nst_ref, y_t_ref = refs[-2:]
    h = pl.program_id(0)
    lane = lax.broadcasted_iota(jnp.int32, (SSD_HEAD_DIM, n), 1)
    x_hi, x_lo = _split_bf16(xdt_t_ref[...])
    b_hi, b_lo = _split_bf16(b_ref[...])
    b_cat = jnp.concatenate([b_hi, b_lo, b_hi], axis=0)
    c_bf = c_ref[...].astype(BF16)
    zero = jnp.zeros((SSD_HEAD_DIM, n), BF16)
    y_t = jnp.zeros((SSD_HEAD_DIM, n), F32)
    for b0 in range(0, n, bg):
        hi_m = jnp.concatenate([jnp.where(lane == b0 + j, x_hi, zero) for j in range(bg)], axis=0)
        lo_m = jnp.concatenate([jnp.where(lane == b0 + j, x_lo, zero) for j in range(bg)], axis=0)
        upd = _dot(jnp.concatenate([hi_m, hi_m, lo_m], axis=1), b_cat)
        hn = []
        for j in range(bg):
            hj = st_ref[b0 + j] * decay_ref[b0 + j, h] + upd[j * SSD_HEAD_DIM:(j + 1) * SSD_HEAD_DIM, :]
            nst_ref[b0 + j] = hj
            hn.append(hj.astype(BF16))
        yy = lax.dot_general(jnp.concatenate(hn, axis=0), c_bf, (((1,), (1,)), ((), ())),
                             preferred_element_type=F32)
        for j in range(bg):
            y_t = jnp.where(lane == b0 + j, yy[j * SSD_HEAD_DIM:(j + 1) * SSD_HEAD_DIM, :], y_t)
    y_t_ref[...] = y_t


def _sample_ssd_state(st, layer, decay, xdt_t, bc, stacked, bg=8):
    n = st.shape[1]
    hpg = SSD_HEADS // SSD_GROUPS
    col = pl.BlockSpec((SSD_HEAD_DIM, n), lambda h: (h, 0))
    state_block = pl.BlockSpec((None, n, None, SSD_HEAD_DIM, SSD_STATE), lambda h: (layer, 0, h, 0, 0))
    in_specs = [pl.BlockSpec(memory_space=pltpu.SMEM), state_block, col,
                pl.BlockSpec((n, SSD_STATE), lambda h: (0, h // hpg)),
                pl.BlockSpec((n, SSD_STATE), lambda h: (0, SSD_GROUPS + h // hpg))]
    args = [decay, st, xdt_t, bc, bc]
    aliases = {}
    if stacked is not None:
        in_specs.append(pl.BlockSpec(memory_space=pl.ANY))
        args.append(stacked)
        aliases = {len(args) - 1: 0}
    return pl.pallas_call(
        functools.partial(_sample_ssd_state_kernel, n, bg),
        grid=(SSD_HEADS,),
        in_specs=in_specs,
        out_specs=[state_block, col],
        out_shape=[jax.ShapeDtypeStruct(st.shape, F32), jax.ShapeDtypeStruct((SSD_W, n), F32)],
        input_output_aliases=aliases,
        compiler_params=_cparams(1),
        name="sample_ssd_state",
    )(*args)


def _sample_mix_out_kernel(y_t_ref, xs_ref, z_ref, a_ref, x_ref, wo_ref, dx_ref, ng_ref, g_ref, b_ref, o_ref):
    y = y_t_ref[...].T + dx_ref[...] * xs_ref[...]
    y = y * _silu(z_ref[...])
    ms = jnp.mean(y * y, -1, keepdims=True)
    y = y * lax.rsqrt(ms + LN_EPS) * ng_ref[...]
    acc = ALPHA * x_ref[...] + _dot(a_ref[...], wo_ref[0:CONF_W, :]) + _dot(y.astype(BF16), wo_ref[CONF_W:, :])
    o_ref[...] = _layer_norm(acc, g_ref[...], b_ref[...])


def _sample_mix_out(y_t, xs, h, a, x, wo, layer, dx, ng, g, b):
    n = x.shape[0]
    full = lambda arr: pl.BlockSpec(arr.shape, lambda i: (0,) * arr.ndim)
    return pl.pallas_call(
        _sample_mix_out_kernel,
        grid=(1,),
        in_specs=[full(y_t), full(xs), pl.BlockSpec((n, SSD_W), lambda i: (0, C_Z // SSD_W)), full(a),
                  full(x), _layer_resident(wo, layer), full(dx), full(ng), full(g), full(b)],
        out_specs=full(x),
        out_shape=jax.ShapeDtypeStruct(x.shape, F32),
        compiler_params=_cparams(1),
        name="sample_mix_out",
    )(y_t, xs, h, a, x, wo, dx, ng, g, b)


def _sample_attn_kernel(bb, q_ref, k_ref, v_ref, o_ref):
    nhalf = XA_HEAD_DIM // LANES
    rows_per_m = nhalf * XA_HEADS
    for j in range(bb):
        outs = []
        for hd in range(XA_HEADS):
            prod = None
            for half in range(nhalf):
                kk = k_ref[j, pl.ds(half * XA_HEADS + hd, MEM_LEN, stride=rows_per_m), :]
                lo = hd * XA_HEAD_DIM + half * LANES
                part = kk * q_ref[j, :, lo:lo + LANES]
                prod = part if prod is None else prod + part
            s = jnp.sum(prod, axis=-1, keepdims=True) * (XA_HEAD_DIM ** -0.5)
            e = jnp.exp(s - jnp.max(s, axis=0, keepdims=True))
            p = e / jnp.sum(e, axis=0, keepdims=True)
            for half in range(nhalf):
                vv = v_ref[j, pl.ds(half * XA_HEADS + hd, MEM_LEN, stride=rows_per_m), :]
                outs.append(jnp.sum(p * vv, axis=0, keepdims=True))
        o_ref[j] = jnp.concatenate(outs, axis=-1)


def _sample_attn(q, mk, mv, layer, bb=4):
    n, d = q.shape
    row = pl.BlockSpec((bb, 1, d), lambda i: (i, 0, 0))
    mem = pl.BlockSpec((None, bb, mk.shape[2], LANES), lambda i: (layer, i, 0, 0))
    o = pl.pallas_call(
        functools.partial(_sample_attn_kernel, bb),
        grid=(n // bb,),
        in_specs=[row, mem, mem],
        out_specs=row,
        out_shape=jax.ShapeDtypeStruct((n, 1, d), F32),
        compiler_params=_cparams(1),
        name="sample_attn",
    )(q.reshape(n, 1, d), mk, mv)
    return o.reshape(n, d)


FFN_LT = 2 * D_FF // LANES
FFN_ROWS = (FFN_CONV - 1) * FFN_LT


def _flat_ffn_state(s):
    nl, n = s.shape[0], s.shape[1]
    s = s.reshape(nl, n, FFN_CONV - 1, FFN_LT, LANES).transpose(0, 1, 3, 2, 4)
    return s.reshape(nl, n * FFN_ROWS, LANES)


def _unflat_ffn_state(s):
    nl, n = s.shape[0], s.shape[1] // FFN_ROWS
    s = s.reshape(nl, n, FFN_LT, FFN_CONV - 1, LANES).transpose(0, 1, 3, 2, 4)
    return s.reshape(nl, n, FFN_CONV - 1, 2 * D_FF)


def _sample_ffn_kernel(n, u_ref, st_ref, cw_ref, cb_ref, x_ref, wdn_ref, g_ref, b_ref, *refs):
    o_ref, nst_ref = refs[-2:]
    nk = FFN_CONV - 1
    conv = []
    for lt in range(FFN_LT):
        ls = slice(lt * LANES, (lt + 1) * LANES)
        u = u_ref[:, ls]
        hist = [st_ref[pl.ds(nk * lt + k, n, stride=FFN_ROWS), :] for k in range(nk)]
        cv = cb_ref[:, ls] + cw_ref[nk:nk + 1, ls] * u
        for k in range(nk):
            cv = cv + cw_ref[k:k + 1, ls] * hist[k]
        conv.append(cv)
        for k in range(nk - 1):
            nst_ref[pl.ds(nk * lt + k, n, stride=FFN_ROWS), :] = hist[k + 1]
        nst_ref[pl.ds(nk * lt + nk - 1, n, stride=FFN_ROWS), :] = u
    half = FFN_LT // 2
    f = jnp.concatenate([_silu(conv[half + j]) * conv[j] for j in range(half)], axis=-1)
    acc = ALPHA * x_ref[...] + _dot(f.astype(BF16), wdn_ref[...])
    o_ref[...] = _layer_norm(acc, g_ref[...], b_ref[...])


def _sample_ffn(u, st, layer, cw, cb, x, wdn, g, b, stacked):
    n = x.shape[0]
    one = lambda arr: pl.BlockSpec(arr.shape, lambda i: (0,) * arr.ndim, pipeline_mode=pl.Buffered(1))
    state_block = pl.BlockSpec((None, n * FFN_ROWS, LANES), lambda i: (layer, 0, 0), pipeline_mode=pl.Buffered(1))
    in_specs = [one(u), state_block, one(cw), one(cb), one(x), _layer_resident(wdn, layer), one(g), one(b)]
    args = [u, st, cw, cb, x, wdn, g, b]
    aliases = {}
    if stacked is not None:
        in_specs.append(pl.BlockSpec(memory_space=pl.ANY))
        args.append(stacked)
        aliases = {len(args) - 1: 1}
    return pl.pallas_call(
        functools.partial(_sample_ffn_kernel, n),
        grid=(1,),
        in_specs=in_specs,
        out_specs=[pl.BlockSpec(x.shape, lambda i: (0, 0)),
                   pl.BlockSpec((None, n * FFN_ROWS, LANES), lambda i: (layer, 0, 0))],
        out_shape=[jax.ShapeDtypeStruct(x.shape, F32), jax.ShapeDtypeStruct(st.shape, F32)],
        input_output_aliases=aliases,
        compiler_params=_cparams(1),
        name="sample_ffn",
    )(*args)


def kernel(x_prompt, x_sample, cache_mem_k, cache_mem_v, state_conf_conv, state_ssd_conv, state_ssd,
           state_ffn_conv, mem_prompt, w_in, conf_conv_w, conf_conv_b, conf_ln_g, conf_ln_b, ssd_conv_w,
           ssd_conv_b, ssd_dt_bias, ssd_a_log, ssd_d, ssd_norm_g, w_out, ln_mix_g, ln_mix_b, xa_wq, xa_wk,
           xa_wv, xa_wo, ln_xa_g, ln_xa_b, ffn_w_up, ffn_conv_w, ffn_conv_b, ffn_w_down, ln_ffn_g, ln_ffn_b):
    bp, t, d = x_prompt.shape
    ns = x_sample.shape[0]
    nl = DEPTH
    tt_mix, tt_xa, tt_ffn = min(MIXER_TILE, t), min(XATTN_TILE, t), min(FFN_TILE, t)

    row = lambda p, l: p[l][None, :]
    pad_heads = lambda p, l: jnp.pad(p[l], (0, LANES - SSD_HEADS))[None, :]
    rexp = jnp.pad(jnp.repeat(jnp.eye(SSD_HEADS, dtype=F32), SSD_HEAD_DIM, axis=1),
                   ((0, LANES - SSD_HEADS), (0, 0)))

    win = jnp.pad(w_in, ((0, 0), (0, 0), (0, D_IN_PAD - w_in.shape[2]))).astype(BF16)
    wout, wq, wo = w_out.astype(BF16), xa_wq.astype(BF16), xa_wo.astype(BF16)
    wup, wdn = ffn_w_up.astype(BF16), ffn_w_down.astype(BF16)

    mem2d = mem_prompt.reshape(bp * MEM_LEN, d)
    mk_p = _mem_proj(mem2d, xa_wk.astype(BF16), 512).reshape(nl, bp, MEM_ROWS, LANES)
    mv_p = _mem_proj(mem2d, xa_wv.astype(BF16), 512).reshape(nl, bp, MEM_ROWS, LANES)
    mk_s, mv_s = _flat_cache(cache_mem_k), _flat_cache(cache_mem_v)
    conf_s = state_conf_conv.transpose(0, 2, 1, 3)
    ssdc_s = state_ssd_conv.transpose(0, 2, 1, 3)
    ffn_s = _flat_ffn_state(state_ffn_conv)

    xp = _permute_rows(x_prompt, inverse=False)
    xs_ = x_sample.reshape(ns, d)
    st_p = ([], [], [], [])
    nss_all = []
    ncs = nst = nfs = None
    for l in range(nl):
        cw, cb = conf_conv_w[l], row(conf_conv_b, l)
        cg, cbeta = row(conf_ln_g, l), row(conf_ln_b, l)
        sw, sb = ssd_conv_w[l], row(ssd_conv_b, l)
        dtb, alog = pad_heads(ssd_dt_bias, l), pad_heads(ssd_a_log, l)
        dx = jnp.repeat(ssd_d[l], SSD_HEAD_DIM)[None, :]
        ng = row(ssd_norm_g, l)
        g1, b1 = row(ln_mix_g, l), row(ln_mix_b, l)
        g2, b2 = row(ln_xa_g, l), row(ln_xa_b, l)
        g3, b3 = row(ln_ffn_g, l), row(ln_ffn_b, l)
        fcw, fcb = ffn_conv_w[l], row(ffn_conv_b, l)

        x1, cst, sst, hst = _prompt_mixer(xp, l, win, wout, cw, cb, cg, cbeta, sw, sb, dtb, alog, dx, ng, g1, b1,
                                          tt_mix)
        x2 = _prompt_xattn(x1, mk_p, mv_p, l, wq, wo, g2, b2, tt_xa)
        xp, fst = _prompt_ffn(x2, l, wup, wdn, fcw, fcb, g3, b3, tt_ffn)
        for lst, s in zip(st_p, (cst, sst, hst, fst)):
            lst.append(s)

        h = _mm_k(xs_, win, l)
        a_s, ncs = _sample_conf(h, conf_s, l, cw, cb, cg, cbeta, ncs)
        nss, xs1, bc, xdt_t, decay = _sample_ssd_prep(h, ssdc_s, l, sw, sb, dtb, alog, rexp)
        nst, y_t = _sample_ssd_state(state_ssd, l, decay[:, :SSD_HEADS], xdt_t, bc, nst)
        x1s = _sample_mix_out(y_t, xs1, h, a_s, xs_, wout, l, dx, ng, g1, b1)
        qs = _mm_k(x1s, wq, l)
        os_ = _sample_attn(qs, mk_s, mv_s, l)
        x2s = _proj_ln([os_], [wo], l, x1s, g2, b2, ns)
        us = _mm_k(x2s, wup, l)
        xs_, nfs = _sample_ffn(us, ffn_s, l, fcw, fcb, x2s, wdn, g3, b3, nfs)
        nss_all.append(nss)

    st_p = [jnp.stack(s) for s in st_p]
    return (_permute_rows(xp, inverse=True), xs_.reshape(ns, 1, d), st_p[0], st_p[1], st_p[2], st_p[3],
            _unflat_cache(mk_p), _unflat_cache(mv_p),
            ncs.transpose(0, 2, 1, 3), jnp.stack(nss_all).transpose(0, 2, 1, 3), nst, _unflat_ffn_state(nfs))
```

```python
import functools

import jax
import jax.numpy as jnp
from jax import lax
from jax.experimental import pallas as pl
from jax.experimental.pallas import tpu as pltpu

F32 = jnp.float32
BF16 = jnp.bfloat16
HIGHEST = lax.Precision.HIGHEST

D_MODEL = 1024
DEPTH = 4
CONF_W = 1024
CONF_KERNEL = 31
SSD_HEAD_DIM = 64
SSD_HEADS = 16
SSD_W = 1024
SSD_GROUPS = 2
SSD_STATE = 128
SSD_CONV = 4
SSD_CHUNK = 128
XBC_W = SSD_W + 2 * SSD_GROUPS * SSD_STATE
MEM_LEN = 256
XA_HEADS = 4
XA_HEAD_DIM = 256
D_FF = 2816
FFN_CONV = 3
ALPHA = (2.0 * DEPTH) ** 0.25
LN_EPS = 1e-5

LANES = 128
C_GLU, C_Z, C_XBC, C_DT = 0, 2 * CONF_W, 2 * CONF_W + SSD_W, 2 * CONF_W + SSD_W + XBC_W
D_IN_PAD = C_DT + LANES
PROJ_COLS = 256
MIXER_TILE = 256
XATTN_TILE = 1024
FFN_TILE = 1024
FFN_COLS = 256
VMEM_LIMIT = 56 * 1024 * 1024


def _cparams(n_grid_dims):
    return pltpu.CompilerParams(dimension_semantics=("arbitrary",) * n_grid_dims,
                                vmem_limit_bytes=VMEM_LIMIT)


def _resident(shape):
    nd = len(shape)
    return pl.BlockSpec(shape, lambda *_: (0,) * nd, pipeline_mode=pl.Buffered(1))


def _layer_resident(w, layer):
    return pl.BlockSpec((None,) + w.shape[1:], lambda *_: (layer, 0, 0), pipeline_mode=pl.Buffered(1))


def _sigmoid(x):
    return jax.nn.sigmoid(x)


def _silu(x):
    return x * _sigmoid(x)


def _softplus(x):
    return jnp.maximum(x, 0.0) + jnp.log(1.0 + jnp.exp(-jnp.abs(x)))


def _layer_norm(x, g, b):
    mu = jnp.mean(x, -1, keepdims=True)
    xc = x - mu
    var = jnp.mean(xc * xc, -1, keepdims=True)
    return xc * lax.rsqrt(var + LN_EPS) * g + b


def _dot(a, b):
    return jnp.dot(a, b, preferred_element_type=F32)


def _dot_f32(a, b):
    return jnp.dot(a, b, preferred_element_type=F32, precision=HIGHEST)


CHUNK = SSD_CHUNK
SLABS = CHUNK // 8


def _permuted_row(token):
    return 8 * (token % SLABS) + token // SLABS


def _wrapped_slab(cur, prev, shift, sub):
    return jnp.where(sub >= shift, pltpu.roll(cur, shift, 0), pltpu.roll(prev, shift, 0))


def _fill_history(src, base, eseq, nk, width):
    sub = lax.broadcasted_iota(jnp.int32, (8, width), 0)
    for i in range(SLABS):
        for shift in (1, 2):
            p = i - SLABS * shift + (nk - 1)
            if 0 <= p < nk - 1:
                cur = src[base + 8 * i:base + 8 * (i + 1), :]
                prev = src[base - CHUNK + 8 * i:base - CHUNK + 8 * (i + 1), :]
                eseq[8 * p:8 * (p + 1), :] = _wrapped_slab(cur, prev, shift, sub)
    eseq[8 * (nk - 1):8 * (nk - 1) + CHUNK, :] = src[base:base + CHUNK, :]


def _conv_from_history(eseq, w_ref, nk, lo, rows):
    r0, nr = rows
    acc = w_ref[0:1, lo:lo + LANES] * eseq[r0:r0 + nr, lo:lo + LANES]
    for k in range(1, nk):
        acc = acc + w_ref[k:k + 1, lo:lo + LANES] * eseq[8 * k + r0:8 * k + r0 + nr, lo:lo + LANES]
    return acc


def _permute_rows_kernel(tt, inverse, *refs):
    in_refs, o_ref = refs[:-1], refs[-1]
    for c in range(tt // CHUNK):
        for lt, x_ref in enumerate(in_refs):
            ls = slice(lt * LANES, (lt + 1) * LANES)
            if not inverse:
                for i in range(SLABS):
                    o_ref[0, c * CHUNK + 8 * i:c * CHUNK + 8 * (i + 1), ls] = \
                        x_ref[0, pl.ds(c * CHUNK + i, 8, stride=SLABS), :]
            else:
                for s in range(8):
                    for i0 in range(0, SLABS, 8):
                        tok = c * CHUNK + s * SLABS + i0
                        o_ref[0, tok:tok + 8, ls] = x_ref[0, pl.ds(c * CHUNK + 8 * i0 + s, 8, stride=8), :]


def _permute_rows(x, inverse, tt=512):
    bsz, t, w = x.shape
    tt = min(tt, t)
    lane_tile = lambda lt: pl.BlockSpec((1, tt, LANES), lambda bi, i: (bi, i, lt))
    return pl.pallas_call(
        functools.partial(_permute_rows_kernel, tt, inverse),
        grid=(bsz, t // tt),
        in_specs=[lane_tile(lt) for lt in range(w // LANES)],
        out_specs=pl.BlockSpec((1, tt, w), lambda bi, i: (bi, i, 0)),
        out_shape=jax.ShapeDtypeStruct(x.shape, x.dtype),
        compiler_params=_cparams(2),
        name="permute_rows",
    )(*([x] * (w // LANES)))


def _permuted_time(n):
    r = lax.broadcasted_iota(jnp.int32, (CHUNK, CHUNK), n)
    return (r & 7) * SLABS + (r >> 3)


def _prompt_mixer_kernel(tt, x_ref, win_ref, wout_ref, cw_ref, cb_ref, cg_ref, cbeta_ref, sw_ref, sb_ref,
                         dtb_ref, alog_ref, dx_ref, ng_ref, g_ref, b_ref,
                         o_ref, cst_ref, sst_ref, hst_ref,
                         abuf, xbuf, zbuf, dtbuf, eseq, eseq2, cvbuf, xact, ygbuf, abf, ybf, ht):
    t = pl.program_id(1)
    nch = tt // CHUNK

    @pl.when(t == 0)
    def _():
        abuf[0:CHUNK, :] = jnp.zeros((CHUNK, CONF_W), F32)
        xbuf[0:CHUNK, :] = jnp.zeros((CHUNK, XBC_W), F32)
        ht[...] = jnp.zeros(ht.shape, F32)

    xb = x_ref[0].astype(BF16)
    nb = PROJ_COLS
    for j in range(CONF_W // nb):
        val = _dot(xb, win_ref[:, C_GLU + nb * j:C_GLU + nb * (j + 1)])
        gate = _dot(xb, win_ref[:, C_GLU + CONF_W + nb * j:C_GLU + CONF_W + nb * (j + 1)])
        abuf[CHUNK:CHUNK + tt, j * nb:(j + 1) * nb] = val * _sigmoid(gate)
    for j in range(SSD_W // nb):
        zbuf[:, j * nb:(j + 1) * nb] = _dot(xb, win_ref[:, C_Z + j * nb:C_Z + (j + 1) * nb])
    for j in range(XBC_W // nb):
        xbuf[CHUNK:CHUNK + tt, j * nb:(j + 1) * nb] = _dot(xb, win_ref[:, C_XBC + j * nb:C_XBC + (j + 1) * nb])
    dtbuf[...] = _softplus(_dot(xb, win_ref[:, C_DT:C_DT + LANES]) + dtb_ref[...])

    tl, ts = _permuted_time(0), _permuted_time(1)
    causal = tl >= ts
    causal_f = causal.astype(F32)
    a_neg = -jnp.exp(alog_ref[...])
    first_head = lax.broadcasted_iota(jnp.int32, (CHUNK, LANES), 1) < SSD_HEAD_DIM
    rb = 32

    for c in range(nch):
        base = CHUNK * (c + 1)
        r0 = CHUNK * c
        _fill_history(abuf, base, eseq, CONF_KERNEL, CONF_W)
        for lt in range(CONF_W // LANES):
            for r in range(CHUNK // rb):
                cvbuf[r * rb:(r + 1) * rb, lt * LANES:(lt + 1) * LANES] = _conv_from_history(
                    eseq, cw_ref, CONF_KERNEL, lt * LANES, (r * rb, rb))
        y = _layer_norm(cvbuf[...] + cb_ref[...], cg_ref[...], cbeta_ref[...])
        abf[r0:r0 + CHUNK, :] = _silu(y).astype(BF16)

        _fill_history(xbuf, base, eseq2, SSD_CONV, XBC_W)
        for lt in range(XBC_W // LANES):
            lo = lt * LANES
            acc = _conv_from_history(eseq2, sw_ref, SSD_CONV, lo, (0, CHUNK))
            xact[:, lo:lo + LANES] = _silu(acc + sb_ref[:, lo:lo + LANES])

        dt = dtbuf[r0:r0 + CHUNK, :]
        acum = _dot_f32(causal_f, dt * a_neg)
        act = acum.T
        ss = jnp.zeros((CHUNK, 1), F32)
        for g in range(SSD_GROUPS):
            bm = xact[:, SSD_W + g * SSD_STATE:SSD_W + (g + 1) * SSD_STATE]
            cmb = xact[:, SSD_W + (SSD_GROUPS + g) * SSD_STATE:SSD_W + (SSD_GROUPS + g + 1) * SSD_STATE].astype(BF16)
            cb = lax.dot_general(cmb, bm.astype(BF16), (((1,), (1,)), ((), ())), preferred_element_type=F32)
            bmt = bm.T.astype(BF16)
            for jj in range(SSD_W // SSD_GROUPS // LANES):
                lo = (g * (SSD_W // SSD_GROUPS // LANES) + jj) * LANES
                h0 = lo // SSD_HEAD_DIM
                dtl = jnp.where(first_head, dt[:, h0:h0 + 1], dt[:, h0 + 1:h0 + 2])
                acl = jnp.where(first_head, acum[:, h0:h0 + 1], acum[:, h0 + 1:h0 + 2])
                xs = xact[:, lo:lo + LANES]
                dtx = xs * dtl
                dtxb = dtx.astype(BF16)
                a_last = acl[CHUNK - 1:CHUNK, :]
                hprev = ht[:, lo:lo + LANES]
                y_off = _dot(cmb, hprev.astype(BF16)) * jnp.exp(acl)
                yd = []
                for h in (h0, h0 + 1):
                    decay = jnp.exp(jnp.where(causal, acum[:, h:h + 1] - act[h:h + 1, :], -1e30))
                    yd.append(_dot((cb * decay).astype(BF16), dtxb))
                y = jnp.where(first_head, yd[0], yd[1]) + y_off + dx_ref[:, lo:lo + LANES] * xs
                yg = y * _silu(zbuf[r0:r0 + CHUNK, lo:lo + LANES])
                ygbuf[:, lo:lo + LANES] = yg
                ss = ss + jnp.sum(yg * yg, -1, keepdims=True)
                ht[:, lo:lo + LANES] = hprev * jnp.exp(a_last) + _dot(bmt, (dtx * jnp.exp(a_last - acl)).astype(BF16))
        inv = lax.rsqrt(ss * (1.0 / SSD_W) + LN_EPS)
        ybf[r0:r0 + CHUNK, :] = (ygbuf[...] * inv * ng_ref[...]).astype(BF16)

    mix = _dot(abf[...], wout_ref[0:CONF_W, :]) + _dot(ybf[...], wout_ref[CONF_W:, :])
    o_ref[0] = _layer_norm(ALPHA * x_ref[0] + mix, g_ref[...], b_ref[...])

    @pl.when(t == pl.num_programs(1) - 1)
    def _():
        for r in range(CONF_KERNEL - 1):
            src_row = tt + _permuted_row(CHUNK - (CONF_KERNEL - 1) + r)
            cst_ref[0, r:r + 1, :] = abuf[src_row:src_row + 1, :]
        for r in range(SSD_CONV - 1):
            src_row = tt + _permuted_row(CHUNK - (SSD_CONV - 1) + r)
            sst_ref[0, r:r + 1, :] = xbuf[src_row:src_row + 1, :]
        hst_ref[0] = ht[...].T.reshape(SSD_HEADS, SSD_HEAD_DIM, SSD_STATE)

    abuf[0:CHUNK, :] = abuf[tt:tt + CHUNK, :]
    xbuf[0:CHUNK, :] = xbuf[tt:tt + CHUNK, :]


def _prompt_mixer(x, layer, win, wout, cw, cb, cg, cbeta, sw, sb, dtb, alog, dx, ng, g, b, tt):
    bsz, t, d = x.shape
    tile = pl.BlockSpec((1, tt, d), lambda bi, i: (bi, i, 0))
    per_b = lambda *s: pl.BlockSpec((1,) + s, lambda bi, i: (bi,) + (0,) * len(s))
    consts = (cw, cb, cg, cbeta, sw, sb, dtb, alog, dx, ng, g, b)
    return pl.pallas_call(
        functools.partial(_prompt_mixer_kernel, tt),
        grid=(bsz, t // tt),
        in_specs=[tile, _layer_resident(win, layer), _layer_resident(wout, layer)]
        + [_resident(a.shape) for a in consts],
        out_specs=[tile, per_b(CONF_KERNEL - 1, CONF_W), per_b(SSD_CONV - 1, XBC_W),
                   per_b(SSD_HEADS, SSD_HEAD_DIM, SSD_STATE)],
        out_shape=[jax.ShapeDtypeStruct((bsz, t, d), F32),
                   jax.ShapeDtypeStruct((bsz, CONF_KERNEL - 1, CONF_W), F32),
                   jax.ShapeDtypeStruct((bsz, SSD_CONV - 1, XBC_W), F32),
                   jax.ShapeDtypeStruct((bsz, SSD_HEADS, SSD_HEAD_DIM, SSD_STATE), F32)],
        scratch_shapes=[pltpu.VMEM((CHUNK + tt, CONF_W), F32),
                        pltpu.VMEM((CHUNK + tt, XBC_W), F32),
                        pltpu.VMEM((tt, SSD_W), F32),
                        pltpu.VMEM((tt, LANES), F32),
                        pltpu.VMEM((8 * (CONF_KERNEL - 1) + CHUNK, CONF_W), F32),
                        pltpu.VMEM((8 * (SSD_CONV - 1) + CHUNK, XBC_W), F32),
                        pltpu.VMEM((CHUNK, CONF_W), F32),
                        pltpu.VMEM((CHUNK, XBC_W), F32),
                        pltpu.VMEM((CHUNK, SSD_W), F32),
                        pltpu.VMEM((tt, CONF_W), BF16),
                        pltpu.VMEM((tt, SSD_W), BF16),
                        pltpu.VMEM((SSD_STATE, SSD_W), F32)],
        compiler_params=_cparams(2),
        name="prompt_mixer",
    )(x, win, wout, *consts)


def _proj_ln_kernel(n_in, *refs):
    in_refs, w_refs = refs[:n_in], refs[n_in:2 * n_in]
    x_ref, g_ref, b_ref, o_ref = refs[2 * n_in:]
    acc = ALPHA * x_ref[...]
    for a_ref, w_ref in zip(in_refs, w_refs):
        acc = acc + _dot(a_ref[...].astype(BF16), w_ref[...])
    o_ref[...] = _layer_norm(acc, g_ref[...], b_ref[...])


def _proj_ln(ins, ws, layer, resid, g, b, tm):
    m, d = resid.shape
    n_in = len(ins)
    rows = lambda a: pl.BlockSpec((tm, a.shape[1]), lambda i: (i, 0))
    return pl.pallas_call(
        functools.partial(_proj_ln_kernel, n_in),
        grid=(m // tm,),
        in_specs=[rows(a) for a in ins] + [_layer_resident(w, layer) for w in ws]
        + [rows(resid), _resident(g.shape), _resident(b.shape)],
        out_specs=rows(resid),
        out_shape=jax.ShapeDtypeStruct((m, d), F32),
        compiler_params=_cparams(1),
        name="proj_ln",
    )(*ins, *ws, resid, g, b)


def _prompt_xattn_kernel(x_ref, k_ref, v_ref, wq_ref, wo_ref, g_ref, b_ref, o_ref):
    x = x_ref[0]
    q = _dot(x.astype(BF16), wq_ref[...])
    acc = ALPHA * x
    for hd in range(XA_HEADS):
        sl = slice(hd * XA_HEAD_DIM, (hd + 1) * XA_HEAD_DIM)
        qh = q[:, sl].astype(BF16)
        kh = _load_flat_head(k_ref, 0, hd).astype(BF16)
        s = lax.dot_general(qh, kh, (((1,), (1,)), ((), ())), preferred_element_type=F32)
        s = s * (XA_HEAD_DIM ** -0.5)
        e = jnp.exp(s - jnp.max(s, -1, keepdims=True))
        p = e / jnp.sum(e, -1, keepdims=True)
        oh = _dot(p.astype(BF16), _load_flat_head(v_ref, 0, hd).astype(BF16))
        acc = acc + _dot(oh.astype(BF16), wo_ref[sl, :])
    o_ref[0] = _layer_norm(acc, g_ref[...], b_ref[...])


def _prompt_xattn(x, mk, mv, layer, wq, wo, g, b, tt):
    bsz, t, d = x.shape
    tile = pl.BlockSpec((1, tt, d), lambda bi, i: (bi, i, 0))
    mem = pl.BlockSpec((None, 1, MEM_ROWS, LANES), lambda bi, i: (layer, bi, 0, 0))
    return pl.pallas_call(
        _prompt_xattn_kernel,
        grid=(bsz, t // tt),
        in_specs=[tile, mem, mem, _layer_resident(wq, layer), _layer_resident(wo, layer), _resident(g.shape),
                  _resident(b.shape)],
        out_specs=tile,
        out_shape=jax.ShapeDtypeStruct((bsz, t, d), F32),
        compiler_params=_cparams(2),
        name="prompt_xattn",
    )(x, mk, mv, wq, wo, g, b)


def _prompt_ffn_kernel(tt, x_ref, wup_ref, wdn_ref, cw_ref, cb_ref, g_ref, b_ref, o_ref, st_ref, hst):
    t = pl.program_id(1)
    nch = tt // CHUNK

    @pl.when(t == 0)
    def _():
        hst[...] = jnp.zeros(hst.shape, F32)

    xb = x_ref[0].astype(BF16)
    sub = lax.broadcasted_iota(jnp.int32, (8, FFN_COLS), 0)
    acc = jnp.zeros((tt, D_MODEL), F32)
    for j in range(D_FF // FFN_COLS):
        conv = []
        for half in range(2):
            c0 = half * D_FF + j * FFN_COLS
            cs = slice(c0, c0 + FFN_COLS)
            u = _dot(xb, wup_ref[:, cs])
            w0, w1, w2 = cw_ref[0:1, cs], cw_ref[1:2, cs], cw_ref[2:3, cs]
            outs = []
            for c in range(nch):
                cur = u[c * CHUNK:(c + 1) * CHUNK, :]
                prev = hst[:, cs] if c == 0 else u[c * CHUNK - 16:c * CHUNK, :]
                e2 = _wrapped_slab(cur[CHUNK - 16:CHUNK - 8, :], prev[0:8, :], 1, sub)
                e1 = _wrapped_slab(cur[CHUNK - 8:CHUNK, :], prev[8:16, :], 1, sub)
                back1 = jnp.concatenate([e1, cur[0:CHUNK - 8, :]], axis=0)
                back2 = jnp.concatenate([e2, e1, cur[0:CHUNK - 16, :]], axis=0)
                outs.append(cb_ref[:, cs] + w2 * cur + w1 * back1 + w0 * back2)
            hst[:, cs] = u[tt - 16:tt, :]
            conv.append(jnp.concatenate(outs, axis=0))
        f = _silu(conv[1]) * conv[0]
        acc = acc + _dot(f.astype(BF16), wdn_ref[j * FFN_COLS:(j + 1) * FFN_COLS, :])
    o_ref[0] = _layer_norm(ALPHA * x_ref[0] + acc, g_ref[...], b_ref[...])
    st_ref[0, 0:1, :] = hst[7:8, :]
    st_ref[0, 1:2, :] = hst[15:16, :]


def _prompt_ffn(x, layer, wup, wdn, cw, cb, g, b, tt):
    bsz, t, d = x.shape
    tile = pl.BlockSpec((1, tt, d), lambda bi, i: (bi, i, 0))
    return pl.pallas_call(
        functools.partial(_prompt_ffn_kernel, tt),
        grid=(bsz, t // tt),
        in_specs=[tile, _layer_resident(wup, layer), _layer_resident(wdn, layer), _resident(cw.shape),
                  _resident(cb.shape), _resident(g.shape), _resident(b.shape)],
        out_specs=[tile, pl.BlockSpec((1, FFN_CONV - 1, 2 * D_FF), lambda bi, i: (bi, 0, 0))],
        out_shape=[jax.ShapeDtypeStruct((bsz, t, d), F32),
                   jax.ShapeDtypeStruct((bsz, FFN_CONV - 1, 2 * D_FF), F32)],
        scratch_shapes=[pltpu.VMEM((16, 2 * D_FF), F32)],
        compiler_params=_cparams(2),
        name="prompt_ffn",
    )(x, wup, wdn, cw, cb, g, b)


MEM_ROWS = MEM_LEN * (XA_HEAD_DIM // LANES) * XA_HEADS


def _flat_cache(c):
    nl, n = c.shape[0], c.shape[1]
    c = c.reshape(nl, n, MEM_LEN, XA_HEADS, XA_HEAD_DIM // LANES, LANES)
    return c.transpose(0, 1, 2, 4, 3, 5).reshape(nl, n, MEM_ROWS, LANES)


def _unflat_cache(c):
    nl, n = c.shape[0], c.shape[1]
    c = c.reshape(nl, n, MEM_LEN, XA_HEAD_DIM // LANES, XA_HEADS, LANES)
    return c.transpose(0, 1, 2, 4, 3, 5).reshape(nl, n, MEM_LEN, XA_HEADS, XA_HEAD_DIM)


def _load_flat_head(ref, idx, hd):
    rows_per_m = MEM_ROWS // MEM_LEN
    return jnp.concatenate([ref[idx, pl.ds(half * XA_HEADS + hd, MEM_LEN, stride=rows_per_m), :]
                            for half in range(XA_HEAD_DIM // LANES)], axis=-1)


def _mem_proj_kernel(tm, x_ref, w_ref, o_ref):
    o = _dot(x_ref[...].astype(BF16), w_ref[...])
    rows_per_m = MEM_ROWS // MEM_LEN
    for hd in range(XA_HEADS):
        for half in range(XA_HEAD_DIM // LANES):
            lo = hd * XA_HEAD_DIM + half * LANES
            o_ref[pl.ds(half * XA_HEADS + hd, tm, stride=rows_per_m), :] = o[:, lo:lo + LANES]


def _mem_proj(mem, w, tm):
    m, d = mem.shape
    nl = w.shape[0]
    rows_per_m = MEM_ROWS // MEM_LEN
    return pl.pallas_call(
        functools.partial(_mem_proj_kernel, tm),
        grid=(nl, m // tm),
        in_specs=[pl.BlockSpec((tm, d), lambda l, i: (i, 0)),
                  pl.BlockSpec((None,) + w.shape[1:], lambda l, i: (l, 0, 0))],
        out_specs=pl.BlockSpec((None, tm * rows_per_m, LANES), lambda l, i: (l, i, 0)),
        out_shape=jax.ShapeDtypeStruct((nl, m * rows_per_m, LANES), F32),
        compiler_params=_cparams(2),
        name="mem_proj",
    )(mem, w)


def _mm_k_kernel(x_ref, w_ref, o_ref):
    @pl.when(pl.program_id(0) == 0)
    def _():
        o_ref[...] = jnp.zeros(o_ref.shape, F32)

    o_ref[...] += _dot(x_ref[...].astype(BF16), w_ref[...])


def _mm_k(x, w, layer, tk=256):
    m, k = x.shape
    n = w.shape[2]
    return pl.pallas_call(
        _mm_k_kernel,
        grid=(k // tk,),
        in_specs=[pl.BlockSpec((m, tk), lambda i: (0, i)), pl.BlockSpec((None, tk, n), lambda i: (layer, i, 0))],
        out_specs=pl.BlockSpec((m, n), lambda i: (0, 0)),
        out_shape=jax.ShapeDtypeStruct((m, n), F32),
        compiler_params=_cparams(1),
        name="mm_k",
    )(x, w)


def _sample_conf_kernel(h_ref, st_ref, cw_ref, cb_ref, g_ref, b_ref, *refs):
    a_ref, nst_ref = refs[-2:]
    a = h_ref[:, :CONF_W] * _sigmoid(h_ref[:, CONF_W:])
    nk = CONF_KERNEL - 1
    acc = cb_ref[...] + cw_ref[nk:nk + 1, :] * a
    for k in range(nk):
        acc = acc + cw_ref[k:k + 1, :] * st_ref[k]
    a_ref[...] = _silu(_layer_norm(acc, g_ref[...], b_ref[...])).astype(BF16)
    for k in range(nk - 1):
        nst_ref[k] = st_ref[k + 1]
    nst_ref[nk - 1] = a


def _sample_conf(h, st, layer, cw, cb, g, b, stacked, bb=32):
    n = h.shape[0]
    nk = CONF_KERNEL - 1
    state_block = pl.BlockSpec((None, nk, bb, CONF_W), lambda i: (layer, 0, i, 0))
    in_specs = [pl.BlockSpec((bb, 2 * CONF_W), lambda i: (i, 0)), state_block,
                _resident(cw.shape), _resident(cb.shape), _resident(g.shape), _resident(b.shape)]
    args = [h, st, cw, cb, g, b]
    aliases = {}
    if stacked is not None:
        in_specs.append(pl.BlockSpec(memory_space=pl.ANY))
        args.append(stacked)
        aliases = {len(args) - 1: 1}
    return pl.pallas_call(
        _sample_conf_kernel,
        grid=(n // bb,),
        in_specs=in_specs,
        out_specs=[pl.BlockSpec((bb, CONF_W), lambda i: (i, 0)), state_block],
        out_shape=[jax.ShapeDtypeStruct((n, CONF_W), BF16), jax.ShapeDtypeStruct(st.shape, F32)],
        input_output_aliases=aliases,
        compiler_params=_cparams(1),
        name="sample_conf",
    )(*args)


def _sample_ssd_prep_kernel(xbc_ref, dt_ref, st_ref, sw_ref, sb_ref, dtb_ref, alog_ref, rexp_ref,
                            nst_ref, xs_ref, bc_ref, xdt_t_ref, decay_ref):
    xr = xbc_ref[...]
    nk = SSD_CONV - 1
    acc = sb_ref[...] + sw_ref[nk:nk + 1, :] * xr
    for k in range(nk):
        acc = acc + sw_ref[k:k + 1, :] * st_ref[k]
    xbc = _silu(acc)
    for k in range(nk - 1):
        nst_ref[k] = st_ref[k + 1]
    nst_ref[nk - 1] = xr
    xs = xbc[:, :SSD_W]
    xs_ref[...] = xs
    bc_ref[...] = xbc[:, SSD_W:]
    dt = _softplus(dt_ref[...] + dtb_ref[...])
    rexp = rexp_ref[...]
    a_neg = -jnp.exp(alog_ref[...])
    xdt_t_ref[...] = (xs * _dot_f32(dt, rexp)).T
    decay_ref[...] = jnp.exp(dt * a_neg)


def _sample_ssd_prep(h, st, layer, sw, sb, dtb, alog, rexp):
    n = h.shape[0]
    nk = SSD_CONV - 1
    full = lambda r, c: pl.BlockSpec((r, c), lambda i: (0, 0))
    return pl.pallas_call(
        _sample_ssd_prep_kernel,
        grid=(1,),
        in_specs=[pl.BlockSpec((n, XBC_W), lambda i: (0, C_XBC // XBC_W)),
                  pl.BlockSpec((n, LANES), lambda i: (0, C_DT // LANES)),
                  pl.BlockSpec((None, nk, n, XBC_W), lambda i: (layer, 0, 0, 0)),
                  full(*sw.shape), full(*sb.shape), full(*dtb.shape), full(*alog.shape), full(*rexp.shape)],
        out_specs=[pl.BlockSpec((nk, n, XBC_W), lambda i: (0, 0, 0)), full(n, SSD_W), full(n, XBC_W - SSD_W),
                   full(SSD_W, n), full(n, LANES)],
        out_shape=[jax.ShapeDtypeStruct((nk, n, XBC_W), F32), jax.ShapeDtypeStruct((n, SSD_W), F32),
                   jax.ShapeDtypeStruct((n, XBC_W - SSD_W), F32), jax.ShapeDtypeStruct((SSD_W, n), F32),
                   jax.ShapeDtypeStruct((n, LANES), F32)],
        compiler_params=_cparams(1),
        name="sample_ssd_prep",
    )(h, h, st, sw, sb, dtb, alog, rexp)


def _split_bf16(x):
    hi = x.astype(BF16)
    return hi, (x - hi.astype(F32)).astype(BF16)


def _sample_ssd_state_kernel(n, bg, decay_ref, st_ref, xdt_t_ref, b_ref, c_ref, *refs):
    nst_ref, y_t_ref = refs[-2:]
    h = pl.program_id(0)
    lane = lax.broadcasted_iota(jnp.int32, (SSD_HEAD_DIM, n), 1)
    x_hi, x_lo = _split_bf16(xdt_t_ref[...])
    b_hi, b_lo = _split_bf16(b_ref[...])
    b_cat = jnp.concatenate([b_hi, b_lo, b_hi], axis=0)
    c_bf = c_ref[...].astype(BF16)
    zero = jnp.zeros((SSD_HEAD_DIM, n), BF16)
    y_t = jnp.zeros((SSD_HEAD_DIM, n), F32)
    for b0 in range(0, n, bg):
        hi_m = jnp.concatenate([jnp.where(lane == b0 + j, x_hi, zero) for j in range(bg)], axis=0)
        lo_m = jnp.concatenate([jnp.where(lane == b0 + j, x_lo, zero) for j in range(bg)], axis=0)
        upd = _dot(jnp.concatenate([hi_m, hi_m, lo_m], axis=1), b_cat)
        hn = []
        for j in range(bg):
            hj = st_ref[b0 + j] * decay_ref[b0 + j, h] + upd[j * SSD_HEAD_DIM:(j + 1) * SSD_HEAD_DIM, :]
            nst_ref[b0 + j] = hj
            hn.append(hj.astype(BF16))
        yy = lax.dot_general(jnp.concatenate(hn, axis=0), c_bf, (((1,), (1,)), ((), ())),
                             preferred_element_type=F32)
        for j in range(bg):
            y_t = jnp.where(lane == b0 + j, yy[j * SSD_HEAD_DIM:(j + 1) * SSD_HEAD_DIM, :], y_t)
    y_t_ref[...] = y_t


def _sample_ssd_state(st, layer, decay, xdt_t, bc, stacked, bg=8):
    n = st.shape[1]
    hpg = SSD_HEADS // SSD_GROUPS
    col = pl.BlockSpec((SSD_HEAD_DIM, n), lambda h: (h, 0))
    state_block = pl.BlockSpec((None, n, None, SSD_HEAD_DIM, SSD_STATE), lambda h: (layer, 0, h, 0, 0))
    in_specs = [pl.BlockSpec(memory_space=pltpu.SMEM), state_block, col,
                pl.BlockSpec((n, SSD_STATE), lambda h: (0, h // hpg)),
                pl.BlockSpec((n, SSD_STATE), lambda h: (0, SSD_GROUPS + h // hpg))]
    args = [decay, st, xdt_t, bc, bc]
    aliases = {}
    if stacked is not None:
        in_specs.append(pl.BlockSpec(memory_space=pl.ANY))
        args.append(stacked)
        aliases = {len(args) - 1: 0}
    return pl.pallas_call(
        functools.partial(_sample_ssd_state_kernel, n, bg),
        grid=(SSD_HEADS,),
        in_specs=in_specs,
        out_specs=[state_block, col],
        out_shape=[jax.ShapeDtypeStruct(st.shape, F32), jax.ShapeDtypeStruct((SSD_W, n), F32)],
        input_output_aliases=aliases,
        compiler_params=_cparams(1),
        name="sample_ssd_state",
    )(*args)


def _sample_mix_out_kernel(y_t_ref, xs_ref, z_ref, a_ref, x_ref, wo_ref, dx_ref, ng_ref, g_ref, b_ref, o_ref):
    y = y_t_ref[...].T + dx_ref[...] * xs_ref[...]
    y = y * _silu(z_ref[...])
    ms = jnp.mean(y * y, -1, keepdims=True)
    y = y * lax.rsqrt(ms + LN_EPS) * ng_ref[...]
    acc = ALPHA * x_ref[...] + _dot(a_ref[...], wo_ref[0:CONF_W, :]) + _dot(y.astype(BF16), wo_ref[CONF_W:, :])
    o_ref[...] = _layer_norm(acc, g_ref[...], b_ref[...])


def _sample_mix_out(y_t, xs, h, a, x, wo, layer, dx, ng, g, b):
    n = x.shape[0]
    full = lambda arr: pl.BlockSpec(arr.shape, lambda i: (0,) * arr.ndim)
    return pl.pallas_call(
        _sample_mix_out_kernel,
        grid=(1,),
        in_specs=[full(y_t), full(xs), pl.BlockSpec((n, SSD_W), lambda i: (0, C_Z // SSD_W)), full(a),
                  full(x), _layer_resident(wo, layer), full(dx), full(ng), full(g), full(b)],
        out_specs=full(x),
        out_shape=jax.ShapeDtypeStruct(x.shape, F32),
        compiler_params=_cparams(1),
        name="sample_mix_out",
    )(y_t, xs, h, a, x, wo, dx, ng, g, b)


def _sample_attn_kernel(bb, q_ref, k_ref, v_ref, o_ref):
    nhalf = XA_HEAD_DIM // LANES
    rows_per_m = nhalf * XA_HEADS
    for j in range(bb):
        outs = []
        for hd in range(XA_HEADS):
            prod = None
            for half in range(nhalf):
                kk = k_ref[j, pl.ds(half * XA_HEADS + hd, MEM_LEN, stride=rows_per_m), :]
                lo = hd * XA_HEAD_DIM + half * LANES
                part = kk * q_ref[j, :, lo:lo + LANES]
                prod = part if prod is None else prod + part
            s = jnp.sum(prod, axis=-1, keepdims=True) * (XA_HEAD_DIM ** -0.5)
            e = jnp.exp(s - jnp.max(s, axis=0, keepdims=True))
            p = e / jnp.sum(e, axis=0, keepdims=True)
            for half in range(nhalf):
                vv = v_ref[j, pl.ds(half * XA_HEADS + hd, MEM_LEN, stride=rows_per_m), :]
                outs.append(jnp.sum(p * vv, axis=0, keepdims=True))
        o_ref[j] = jnp.concatenate(outs, axis=-1)


def _sample_attn(q, mk, mv, layer, bb=4):
    n, d = q.shape
    row = pl.BlockSpec((bb, 1, d), lambda i: (i, 0, 0))
    mem = pl.BlockSpec((None, bb, mk.shape[2], LANES), lambda i: (layer, i, 0, 0))
    o = pl.pallas_call(
        functools.partial(_sample_attn_kernel, bb),
        grid=(n // bb,),
        in_specs=[row, mem, mem],
        out_specs=row,
        out_shape=jax.ShapeDtypeStruct((n, 1, d), F32),
        compiler_params=_cparams(1),
        name="sample_attn",
    )(q.reshape(n, 1, d), mk, mv)
    return o.reshape(n, d)


FFN_LT = 2 * D_FF // LANES
FFN_ROWS = (FFN_CONV - 1) * FFN_LT


def _flat_ffn_state(s):
    nl, n = s.shape[0], s.shape[1]
    s = s.reshape(nl, n, FFN_CONV - 1, FFN_LT, LANES).transpose(0, 1, 3, 2, 4)
    return s.reshape(nl, n * FFN_ROWS, LANES)


def _unflat_ffn_state(s):
    nl, n = s.shape[0], s.shape[1] // FFN_ROWS
    s = s.reshape(nl, n, FFN_LT, FFN_CONV - 1, LANES).transpose(0, 1, 3, 2, 4)
    return s.reshape(nl, n, FFN_CONV - 1, 2 * D_FF)


def _sample_ffn_kernel(n, u_ref, st_ref, cw_ref, cb_ref, x_ref, wdn_ref, g_ref, b_ref, *refs):
    o_ref, nst_ref = refs[-2:]
    nk = FFN_CONV - 1
    conv = []
    for lt in range(FFN_LT):
        ls = slice(lt * LANES, (lt + 1) * LANES)
        u = u_ref[:, ls]
        hist = [st_ref[pl.ds(nk * lt + k, n, stride=FFN_ROWS), :] for k in range(nk)]
        cv = cb_ref[:, ls] + cw_ref[nk:nk + 1, ls] * u
        for k in range(nk):
            cv = cv + cw_ref[k:k + 1, ls] * hist[k]
        conv.append(cv)
        for k in range(nk - 1):
            nst_ref[pl.ds(nk * lt + k, n, stride=FFN_ROWS), :] = hist[k + 1]
        nst_ref[pl.ds(nk * lt + nk - 1, n, stride=FFN_ROWS), :] = u
    half = FFN_LT // 2
    f = jnp.concatenate([_silu(conv[half + j]) * conv[j] for j in range(half)], axis=-1)
    acc = ALPHA * x_ref[...] + _dot(f.astype(BF16), wdn_ref[...])
    o_ref[...] = _layer_norm(acc, g_ref[...], b_ref[...])


def _sample_ffn(u, st, layer, cw, cb, x, wdn, g, b, stacked):
    n = x.shape[0]
    one = lambda arr: pl.BlockSpec(arr.shape, lambda i: (0,) * arr.ndim, pipeline_mode=pl.Buffered(1))
    state_block = pl.BlockSpec((None, n * FFN_ROWS, LANES), lambda i: (layer, 0, 0), pipeline_mode=pl.Buffered(1))
    in_specs = [one(u), state_block, one(cw), one(cb), one(x), _layer_resident(wdn, layer), one(g), one(b)]
    args = [u, st, cw, cb, x, wdn, g, b]
    aliases = {}
    if stacked is not None:
        in_specs.append(pl.BlockSpec(memory_space=pl.ANY))
        args.append(stacked)
        aliases = {len(args) - 1: 1}
    return pl.pallas_call(
        functools.partial(_sample_ffn_kernel, n),
        grid=(1,),
        in_specs=in_specs,
        out_specs=[pl.BlockSpec(x.shape, lambda i: (0, 0)),
                   pl.BlockSpec((None, n * FFN_ROWS, LANES), lambda i: (layer, 0, 0))],
        out_shape=[jax.ShapeDtypeStruct(x.shape, F32), jax.ShapeDtypeStruct(st.shape, F32)],
        input_output_aliases=aliases,
        compiler_params=_cparams(1),
        name="sample_ffn",
    )(*args)


def kernel(x_prompt, x_sample, cache_mem_k, cache_mem_v, state_conf_conv, state_ssd_conv, state_ssd,
           state_ffn_conv, mem_prompt, w_in, conf_conv_w, conf_conv_b, conf_ln_g, conf_ln_b, ssd_conv_w,
           ssd_conv_b, ssd_dt_bias, ssd_a_log, ssd_d, ssd_norm_g, w_out, ln_mix_g, ln_mix_b, xa_wq, xa_wk,
           xa_wv, xa_wo, ln_xa_g, ln_xa_b, ffn_w_up, ffn_conv_w, ffn_conv_b, ffn_w_down, ln_ffn_g, ln_ffn_b):
    bp, t, d = x_prompt.shape
    ns = x_sample.shape[0]
    nl = DEPTH
    tt_mix, tt_xa, tt_ffn = min(MIXER_TILE, t), min(XATTN_TILE, t), min(FFN_TILE, t)

    row = lambda p, l: p[l][None, :]
    pad_heads = lambda p, l: jnp.pad(p[l], (0, LANES - SSD_HEADS))[None, :]
    rexp = jnp.pad(jnp.repeat(jnp.eye(SSD_HEADS, dtype=F32), SSD_HEAD_DIM, axis=1),
                   ((0, LANES - SSD_HEADS), (0, 0)))

    win = jnp.pad(w_in, ((0, 0), (0, 0), (0, D_IN_PAD - w_in.shape[2]))).astype(BF16)
    wout, wq, wo = w_out.astype(BF16), xa_wq.astype(BF16), xa_wo.astype(BF16)
    wup, wdn = ffn_w_up.astype(BF16), ffn_w_down.astype(BF16)

    mem2d = mem_prompt.reshape(bp * MEM_LEN, d)
    mk_p = _mem_proj(mem2d, xa_wk.astype(BF16), 512).reshape(nl, bp, MEM_ROWS, LANES)
    mv_p = _mem_proj(mem2d, xa_wv.astype(BF16), 512).reshape(nl, bp, MEM_ROWS, LANES)
    mk_s, mv_s = _flat_cache(cache_mem_k), _flat_cache(cache_mem_v)
    conf_s = state_conf_conv.transpose(0, 2, 1, 3)
    ssdc_s = state_ssd_conv.transpose(0, 2, 1, 3)
    ffn_s = _flat_ffn_state(state_ffn_conv)

    xp = _permute_rows(x_prompt, inverse=False)
    xs_ = x_sample.reshape(ns, d)
    st_p = ([], [], [], [])
    nss_all = []
    ncs = nst = nfs = None
    for l in range(nl):
        cw, cb = conf_conv_w[l], row(conf_conv_b, l)
        cg, cbeta = row(conf_ln_g, l), row(conf_ln_b, l)
        sw, sb = ssd_conv_w[l], row(ssd_conv_b, l)
        dtb, alog = pad_heads(ssd_dt_bias, l), pad_heads(ssd_a_log, l)
        dx = jnp.repeat(ssd_d[l], SSD_HEAD_DIM)[None, :]
        ng = row(ssd_norm_g, l)
        g1, b1 = row(ln_mix_g, l), row(ln_mix_b, l)
        g2, b2 = row(ln_xa_g, l), row(ln_xa_b, l)
        g3, b3 = row(ln_ffn_g, l), row(ln_ffn_b, l)
        fcw, fcb = ffn_conv_w[l], row(ffn_conv_b, l)

        x1, cst, sst, hst = _prompt_mixer(xp, l, win, wout, cw, cb, cg, cbeta, sw, sb, dtb, alog, dx, ng, g1, b1,
                                          tt_mix)
        x2 = _prompt_xattn(x1, mk_p, mv_p, l, wq, wo, g2, b2, tt_xa)
        xp, fst = _prompt_ffn(x2, l, wup, wdn, fcw, fcb, g3, b3, tt_ffn)
        for lst, s in zip(st_p, (cst, sst, hst, fst)):
            lst.append(s)

        h = _mm_k(xs_, win, l)
        a_s, ncs = _sample_conf(h, conf_s, l, cw, cb, cg, cbeta, ncs)
        nss, xs1, bc, xdt_t, decay = _sample_ssd_prep(h, ssdc_s, l, sw, sb, dtb, alog, rexp)
        nst, y_t = _sample_ssd_state(state_ssd, l, decay[:, :SSD_HEADS], xdt_t, bc, nst)
        x1s = _sample_mix_out(y_t, xs1, h, a_s, xs_, wout, l, dx, ng, g1, b1)
        qs = _mm_k(x1s, wq, l)
        os_ = _sample_attn(qs, mk_s, mv_s, l)
        x2s = _proj_ln([os_], [wo], l, x1s, g2, b2, ns)
        us = _mm_k(x2s, wup, l)
        xs_, nfs = _sample_ffn(us, ffn_s, l, fcw, fcb, x2s, wdn, g3, b3, nfs)
        nss_all.append(nss)

    st_p = [jnp.stack(s) for s in st_p]
    return (_permute_rows(xp, inverse=True), xs_.reshape(ns, 1, d), st_p[0], st_p[1], st_p[2], st_p[3],
            _unflat_cache(mk_p), _unflat_cache(mv_p),
            ncs.transpose(0, 2, 1, 3), jnp.stack(nss_all).transpose(0, 2, 1, 3), nst, _unflat_ffn_state(nfs))
```
